```python
import math
import jax, jax.numpy as jnp
from jax import lax
import numpy as np

D_MODEL = 1024
BATCH = 16
SEQ = 4096
DEPTH = 4

ATT_HEADS = 8
ATT_HEAD_DIM = 64
ATT_W = ATT_HEADS * ATT_HEAD_DIM
Q_BLOCK = 128
SSM_GROUP = 16
SSM_GROUPS = 16
SSM_W = SSM_GROUP * SSM_GROUPS
SSM_STATE = 64
DT_MIN = 1e-3
DT_MAX = 1e-1
CONV_W = 256
CONV_K = 31
N_BRANCH = 3
D_FF = 2816
N_EXPERTS = 8
TOP_K = 2
D_FF_EXPERT = 3584
N_DENSE = (DEPTH + 1) // 2
N_MOE = DEPTH // 2
EPS = 1e-6

Q_END = ATT_W
K_END = 2 * ATT_W
V_END = 3 * ATT_W
F_END = V_END + ATT_HEADS
U_END = F_END + SSM_W
C_END = U_END + 2 * CONV_W
N_IN = C_END + N_BRANCH * D_MODEL
SPLITS = [Q_END, K_END, V_END, F_END, U_END, C_END]

kernel_name = "gated_parallel_fox_s5_conformer_moe_trunk"


def rmsnorm(x, g):
    xf = x.astype(jnp.float32)
    y = xf * lax.rsqrt(jnp.mean(xf * xf, axis=-1, keepdims=True) + EPS) * g.astype(jnp.float32)
    return y.astype(x.dtype)


def fox_attention(q, k, v, logf):
    B, L, H, dh = q.shape
    nb = L // Q_BLOCK
    Fh = jnp.cumsum(logf, axis=1).transpose(0, 2, 1)
    qb = q.astype(jnp.float32).reshape(B, nb, Q_BLOCK, H, dh).transpose(1, 0, 2, 3, 4)
    Fb = Fh.reshape(B, H, nb, Q_BLOCK).transpose(2, 0, 1, 3)
    kf = k.astype(jnp.float32)
    vf = v.astype(jnp.float32)
    kpos = jnp.arange(L)
    scale = dh ** -0.5

    def block(args):
        i, qi, Fi = args
        qpos = i * Q_BLOCK + jnp.arange(Q_BLOCK)
        s = jnp.einsum('bqhd,bkhd->bhqk', qi, kf) * scale
        s = s + Fi[..., None] - Fh[:, :, None, :]
        s = jnp.where(kpos[None, :] <= qpos[:, None], s, -jnp.inf)
        p = jax.nn.softmax(s, axis=-1)
        return jnp.einsum('bhqk,bkhd->bqhd', p, vf)

    out = lax.map(block, (jnp.arange(nb), qb, Fb))
    return out.transpose(1, 0, 2, 3, 4).reshape(B, L, H * dh)


def s5_ssm(u, a_re, a_im, log_dt, b_re, b_im, c_re, c_im, d_skip, w_glu):
    B, L, _ = u.shape
    uf = u.astype(jnp.float32).reshape(B, L, SSM_GROUPS, SSM_GROUP)
    lam = lax.complex(a_re.astype(jnp.float32), a_im.astype(jnp.float32))
    dt = jnp.exp(log_dt.astype(jnp.float32))[:, None]
    a_bar = jnp.exp(lam * dt)
    b = lax.complex(b_re.astype(jnp.float32), b_im.astype(jnp.float32))
    b_bar = ((a_bar - 1.0) / lam)[..., None] * b
    bu = jnp.einsum('blgh,gph->blgp', uf.astype(jnp.complex64), b_bar)
    a_seq = jnp.broadcast_to(a_bar[None, None], (1, L) + a_bar.shape)

    def combine(e1, e2):
        a1, x1 = e1
        a2, x2 = e2
        return a1 * a2, a2 * x1 + x2

    _, xs = lax.associative_scan(combine, (a_seq, bu), axis=1)
    c = lax.complex(c_re.astype(jnp.float32), c_im.astype(jnp.float32))
    y = jnp.real(jnp.einsum('blgp,ghp->blgh', xs, c)) + d_skip.astype(jnp.float32) * uf
    y = jax.nn.gelu(y.reshape(B, L, SSM_W))
    val, gate = jnp.split(y @ w_glu.astype(jnp.float32), 2, axis=-1)
    return val * jax.nn.sigmoid(gate)


def conformer_conv(z, conv_w, conv_b, ln_g, ln_b):
    a, g = jnp.split(z.astype(jnp.float32), 2, axis=-1)
    h = a * jax.nn.sigmoid(g)
    hp = jnp.pad(h, ((0, 0), (CONV_K - 1, 0), (0, 0)))
    y = lax.conv_general_dilated(hp, conv_w.astype(jnp.float32)[:, None, :], window_strides=(1,),
                                 padding='VALID', dimension_numbers=('NWC', 'WIO', 'NWC'),
                                 feature_group_count=CONV_W) + conv_b.astype(jnp.float32)
    mu = jnp.mean(y, axis=-1, keepdims=True)
    var = jnp.mean(jnp.square(y - mu), axis=-1, keepdims=True)
    y = (y - mu) * lax.rsqrt(var + EPS) * ln_g.astype(jnp.float32) + ln_b.astype(jnp.float32)
    return jax.nn.silu(y)


def swiglu(h, w_gate, w_up, w_down):
    return (jax.nn.silu(h @ w_gate) * (h @ w_up)) @ w_down


def moe_swiglu(h, w_router, b_router, w_gate, w_up, w_down):
    B, L, D = h.shape
    t = h.reshape(-1, D)
    logits = (t @ w_router).astype(jnp.float32) + b_router.astype(jnp.float32)
    top_val, top_idx = lax.top_k(logits, TOP_K)
    probs = jax.nn.softmax(top_val, axis=-1)
    weights = jnp.sum(jax.nn.one_hot(top_idx, N_EXPERTS, dtype=jnp.float32) * probs[..., None], axis=1)
    weights = weights.astype(t.dtype)
    out = jnp.zeros_like(t)
    for e in range(N_EXPERTS):
        out = out + weights[:, e:e + 1] * swiglu(t, w_gate[e], w_up[e], w_down[e])
    return out.reshape(B, L, D)


def setup_inputs(seed: int = 0) -> dict:
    key = jax.random.key(seed)
    ks = jax.random.split(key, 32)
    f32 = jnp.float32
    nrm = lambda k, shape, scale: jax.random.normal(k, shape, f32) * scale
    x = jax.random.normal(ks[0], (BATCH, SEQ, D_MODEL), f32)
    g_mix = 1.0 + nrm(ks[1], (DEPTH, D_MODEL), 0.02)
    w_in = nrm(ks[2], (DEPTH, D_MODEL, N_IN), D_MODEL ** -0.5)
    b_in = nrm(ks[3], (DEPTH, N_IN), 0.02)
    b_in = b_in.at[:, V_END:F_END].add(jnp.linspace(1.0, 6.0, ATT_HEADS, dtype=f32))
    n = jnp.arange(SSM_STATE, dtype=f32)
    ssm_a_re = -0.5 + nrm(ks[4], (DEPTH, SSM_GROUPS, SSM_STATE), 0.01)
    ssm_a_im = math.pi * n + nrm(ks[5], (DEPTH, SSM_GROUPS, SSM_STATE), 0.01)
    ssm_log_dt = jax.random.uniform(ks[6], (DEPTH, SSM_GROUPS), f32, math.log(DT_MIN), math.log(DT_MAX))
    bs = (2.0 * SSM_GROUP) ** -0.5
    ssm_b_re = nrm(ks[7], (DEPTH, SSM_GROUPS, SSM_STATE, SSM_GROUP), bs)
    ssm_b_im = nrm(ks[8], (DEPTH, SSM_GROUPS, SSM_STATE, SSM_GROUP), bs)
    cs = (2.0 * SSM_STATE) ** -0.5
    ssm_c_re = nrm(ks[9], (DEPTH, SSM_GROUPS, SSM_GROUP, SSM_STATE), cs)
    ssm_c_im = nrm(ks[10], (DEPTH, SSM_GROUPS, SSM_GROUP, SSM_STATE), cs)
    ssm_d = nrm(ks[11], (DEPTH, SSM_GROUPS, SSM_GROUP), 0.5)
    ssm_w_glu = nrm(ks[12], (DEPTH, SSM_W, 2 * SSM_W), SSM_W ** -0.5)
    conv_w = nrm(ks[13], (DEPTH, CONV_K, CONV_W), CONV_K ** -0.5)
    conv_b = nrm(ks[14], (DEPTH, CONV_W), 0.02)
    conv_ln_g = 1.0 + nrm(ks[15], (DEPTH, CONV_W), 0.02)
    conv_ln_b = nrm(ks[16], (DEPTH, CONV_W), 0.02)
    w_att_out = nrm(ks[17], (DEPTH, ATT_W, D_MODEL), ATT_W ** -0.5)
    w_ssm_out = nrm(ks[18], (DEPTH, SSM_W, D_MODEL), SSM_W ** -0.5)
    w_conv_out = nrm(ks[19], (DEPTH, CONV_W, D_MODEL), CONV_W ** -0.5)
    w_o = nrm(ks[20], (DEPTH, D_MODEL, D_MODEL), D_MODEL ** -0.5)
    g_ffn = 1.0 + nrm(ks[21], (DEPTH, D_MODEL), 0.02)
    ffn_w_gate = nrm(ks[22], (N_DENSE, D_MODEL, D_FF), D_MODEL ** -0.5)
    ffn_w_up = nrm(ks[23], (N_DENSE, D_MODEL, D_FF), D_MODEL ** -0.5)
    ffn_w_down = nrm(ks[24], (N_DENSE, D_FF, D_MODEL), D_FF ** -0.5)
    moe_w_router = nrm(ks[25], (N_MOE, D_MODEL, N_EXPERTS), D_MODEL ** -0.5)
    moe_b_router = nrm(ks[26], (N_MOE, N_EXPERTS), 0.01)
    moe_w_gate = nrm(ks[27], (N_MOE, N_EXPERTS, D_MODEL, D_FF_EXPERT), D_MODEL ** -0.5)
    moe_w_up = nrm(ks[28], (N_MOE, N_EXPERTS, D_MODEL, D_FF_EXPERT), D_MODEL ** -0.5)
    moe_w_down = nrm(ks[29], (N_MOE, N_EXPERTS, D_FF_EXPERT, D_MODEL), D_FF_EXPERT ** -0.5)
    g_final = 1.0 + nrm(ks[30], (D_MODEL,), 0.02)
    return {"x": x, "g_mix": g_mix, "w_in": w_in, "b_in": b_in,
            "ssm_a_re": ssm_a_re, "ssm_a_im": ssm_a_im, "ssm_log_dt": ssm_log_dt,
            "ssm_b_re": ssm_b_re, "ssm_b_im": ssm_b_im, "ssm_c_re": ssm_c_re, "ssm_c_im": ssm_c_im,
            "ssm_d": ssm_d, "ssm_w_glu": ssm_w_glu,
            "conv_w": conv_w, "conv_b": conv_b, "conv_ln_g": conv_ln_g, "conv_ln_b": conv_ln_b,
            "w_att_out": w_att_out, "w_ssm_out": w_ssm_out, "w_conv_out": w_conv_out, "w_o": w_o,
            "g_ffn": g_ffn, "ffn_w_gate": ffn_w_gate, "ffn_w_up": ffn_w_up, "ffn_w_down": ffn_w_down,
            "moe_w_router": moe_w_router, "moe_b_router": moe_b_router,
            "moe_w_gate": moe_w_gate, "moe_w_up": moe_w_up, "moe_w_down": moe_w_down,
            "g_final": g_final}


def reference(x, g_mix, w_in, b_in, ssm_a_re, ssm_a_im, ssm_log_dt, ssm_b_re, ssm_b_im,
              ssm_c_re, ssm_c_im, ssm_d, ssm_w_glu, conv_w, conv_b, conv_ln_g, conv_ln_b,
              w_att_out, w_ssm_out, w_conv_out, w_o, g_ffn, ffn_w_gate, ffn_w_up, ffn_w_down,
              moe_w_router, moe_b_router, moe_w_gate, moe_w_up, moe_w_down, g_final):
    B, L, D = x.shape
    dt = x.dtype
    for l in range(DEPTH):
        h = rmsnorm(x, g_mix[l])
        proj = h @ w_in[l] + b_in[l]
        q, k, v, fz, u, cz, gz = jnp.split(proj, SPLITS, axis=-1)
        hs = (B, L, ATT_HEADS, ATT_HEAD_DIM)
        logf = jax.nn.log_sigmoid(fz.astype(jnp.float32))
        att = fox_attention(q.reshape(hs), k.reshape(hs), v.reshape(hs), logf).astype(dt)
        ssm = s5_ssm(u, ssm_a_re[l], ssm_a_im[l], ssm_log_dt[l], ssm_b_re[l], ssm_b_im[l],
                     ssm_c_re[l], ssm_c_im[l], ssm_d[l], ssm_w_glu[l]).astype(dt)
        conv = conformer_conv(cz, conv_w[l], conv_b[l], conv_ln_g[l], conv_ln_b[l]).astype(dt)
        gates = jax.nn.sigmoid(gz).reshape(B, L, N_BRANCH, D)
        merged = (gates[:, :, 0] * (att @ w_att_out[l])
                  + gates[:, :, 1] * (ssm @ w_ssm_out[l])
                  + gates[:, :, 2] * (conv @ w_conv_out[l]))
        x = x + merged @ w_o[l]
        h = rmsnorm(x, g_ffn[l])
        if l % 2 == 0:
            i = l // 2
            f = swiglu(h, ffn_w_gate[i], ffn_w_up[i], ffn_w_down[i])
        else:
            i = l // 2
            f = moe_swiglu(h, moe_w_router[i], moe_b_router[i], moe_w_gate[i], moe_w_up[i], moe_w_down[i])
        x = x + f.astype(dt)
    return rmsnorm(x, g_final)
```

```python
import functools
import math

import jax
import jax.numpy as jnp
from jax import lax
from jax.experimental import pallas as pl
from jax.experimental.pallas import tpu as pltpu

F32 = jnp.float32
BF16 = jnp.bfloat16
HIGHEST = lax.Precision.HIGHEST

EPS = 1e-6
ATT_HEADS = 8
ATT_HEAD_DIM = 64
SSM_GROUPS = 16
SSM_GROUP = 16
SSM_STATE = 64
CONV_K = 31
N_EXPERTS = 8
LANES = 128
HEAD_PAD = 128
SSM_CHUNK = 8
CONV_HALO = 32
NEG_BIG = -1e30

TOKEN_TILE = 512
ATT_TILE = 512
CONV_TILE = 512
CONV_ROWS = 64
MOE_TILE = 512
VMEM_LIMIT = 56 * 1024 * 1024


def _resident(shape):
    nd = len(shape)
    return pl.BlockSpec(shape, lambda *_: (0,) * nd, pipeline_mode=pl.Buffered(1))


def _sigmoid(x):
    return 1.0 / (1.0 + jnp.exp(-x))


def _silu(x):
    return x * _sigmoid(x)


def _gelu_tanh(x):
    c = math.sqrt(2.0 / math.pi)
    return 0.5 * x * (1.0 + jnp.tanh(c * (x + 0.044715 * (x * x * x))))


def _rms(x, g):
    return x * lax.rsqrt(jnp.mean(x * x, axis=-1, keepdims=True) + EPS) * g


def _bdot(a, b):
    return jnp.dot(a, b, preferred_element_type=F32)


def _inproj_kernel(x_ref, g_ref, wqkv_ref, bqkv_ref, wf_ref, bf_ref, wu_ref, bu_ref,
                   wc_ref, bc_ref, wg_ref, bg_ref,
                   qkv_ref, logf_ref, u_ref, hc_ref, gates_ref):
    h = _rms(x_ref[...], g_ref[...]).astype(BF16)
    qkv_ref[...] = (_bdot(h, wqkv_ref[...]) + bqkv_ref[...]).astype(BF16)
    fz = _bdot(h, wf_ref[...]) + bf_ref[...]
    logf_ref[...] = jnp.minimum(fz, 0.0) - jnp.log1p(jnp.exp(-jnp.abs(fz)))
    u_ref[...] = (_bdot(h, wu_ref[...]) + bu_ref[...]).astype(BF16)
    cz = _bdot(h, wc_ref[...]) + bc_ref[...]
    cw = cz.shape[-1] // 2
    hc_ref[...] = (cz[:, :cw] * _sigmoid(cz[:, cw:])).astype(BF16)
    d = x_ref.shape[-1]
    for j in range(gates_ref.shape[-1] // d):
        gz = _bdot(h, wg_ref[:, j * d:(j + 1) * d]) + bg_ref[:, j * d:(j + 1) * d]
        gates_ref[:, j * d:(j + 1) * d] = _sigmoid(gz).astype(BF16)


def _inproj(x2, g, wqkv, bqkv, wf, bf, wu, bu, wc, bc, wg, bg):
    n, d = x2.shape
    tm = min(TOKEN_TILE, n)
    row = lambda w: pl.BlockSpec((tm, w), lambda i: (i, 0))
    outs = [(wqkv.shape[1], BF16), (LANES, F32), (wu.shape[1], BF16), (wc.shape[1] // 2, BF16),
            (wg.shape[1], BF16)]
    return pl.pallas_call(
        _inproj_kernel,
        grid=(n // tm,),
        in_specs=[row(d)] + [_resident(a.shape) for a in (g, wqkv, bqkv, wf, bf, wu, bu, wc, bc, wg, bg)],
        out_specs=[row(w) for w, _ in outs],
        out_shape=[jax.ShapeDtypeStruct((n, w), dt) for w, dt in outs],
        compiler_params=pltpu.CompilerParams(dimension_semantics=("parallel",),
                                             vmem_limit_bytes=VMEM_LIMIT),
        name="inproj",
    )(x2, g, wqkv, bqkv, wf, bf, wu, bu, wc, bc, wg, bg)


def _split3(f):
    hi = f.astype(BF16).astype(F32)
    r = f - hi
    mid = r.astype(BF16).astype(F32)
    lo = (r - mid).astype(BF16).astype(F32)
    return hi, mid, lo


def _foxprep_kernel(qkv_ref, logf_ref, qa_ref, ka_ref, va_ref, carry_ref):
    t = qkv_ref.shape[0]

    @pl.when(pl.program_id(1) == 0)
    def _():
        carry_ref[...] = jnp.zeros_like(carry_ref)

    r = lax.broadcasted_iota(jnp.int32, (t, t), 0)
    c = lax.broadcasted_iota(jnp.int32, (t, t), 1)
    tri = (c <= r).astype(F32)
    fcum = jnp.dot(tri, logf_ref[...], preferred_element_type=F32, precision=HIGHEST) + carry_ref[...]
    carry_ref[...] = fcum[t - 1:t, :]

    dh = ATT_HEAD_DIM
    aw = ATT_HEADS * dh
    lane = lax.broadcasted_iota(jnp.int32, (t, HEAD_PAD - dh), 1)
    for h in range(ATT_HEADS):
        hi, mid, lo = _split3(fcum[:, h:h + 1])
        one = jnp.where(lane < 6, 1.0, 0.0)
        qx = jnp.where(lane == 0, hi, jnp.where(lane == 1, mid, jnp.where(lane == 2, lo, one)))
        kx = jnp.where(lane == 3, -hi, jnp.where(lane == 4, -mid, jnp.where(lane == 5, -lo, one)))
        vx = jnp.where(lane == 0, 1.0, 0.0)
        q = qkv_ref[:, h * dh:(h + 1) * dh].astype(F32)
        k = qkv_ref[:, aw + h * dh:aw + (h + 1) * dh].astype(F32)
        v = qkv_ref[:, 2 * aw + h * dh:2 * aw + (h + 1) * dh].astype(F32)
        qa_ref[0, h] = jnp.concatenate([q, qx], axis=-1).astype(BF16)
        ka_ref[0, h] = jnp.concatenate([k, kx], axis=-1).astype(BF16)
        va_ref[0, h] = jnp.concatenate([v, vx], axis=-1).astype(BF16)


def _foxprep(qkv, logf, b, l):
    t = min(ATT_TILE, l)
    nt = l // t
    out = jax.ShapeDtypeStruct((b, ATT_HEADS, l, HEAD_PAD), BF16)
    ospec = pl.BlockSpec((1, ATT_HEADS, t, HEAD_PAD), lambda bi, i: (bi, 0, i, 0))
    return pl.pallas_call(
        _foxprep_kernel,
        grid=(b, nt),
        in_specs=[pl.BlockSpec((t, qkv.shape[1]), lambda bi, i: (bi * nt + i, 0)),
                  pl.BlockSpec((t, LANES), lambda bi, i: (bi * nt + i, 0))],
        out_specs=[ospec, ospec, ospec],
        out_shape=[out, out, out],
        scratch_shapes=[pltpu.VMEM((1, LANES), F32)],
        compiler_params=pltpu.CompilerParams(dimension_semantics=("parallel", "arbitrary"),
                                             vmem_limit_bytes=VMEM_LIMIT),
        name="foxprep",
    )(qkv, logf)


def _fox_kernel(qi_ref, ki_ref, qa_ref, ka_ref, va_ref, o_ref, m_ref, acc_ref):
    p_id = pl.program_id(1)
    qi = qi_ref[p_id]
    ki = ki_ref[p_id]
    t = qa_ref.shape[2]
    dh = ATT_HEAD_DIM

    @pl.when(ki == 0)
    def _():
        m_ref[...] = jnp.full_like(m_ref, NEG_BIG)
        acc_ref[...] = jnp.zeros_like(acc_ref)

    def step(masked):
        for h in range(ATT_HEADS):
            s = lax.dot_general(qa_ref[0, h], ka_ref[0, h], (((1,), (1,)), ((), ())),
                                preferred_element_type=F32)
            if masked:
                r = lax.broadcasted_iota(jnp.int32, (t, t), 0)
                c = lax.broadcasted_iota(jnp.int32, (t, t), 1)
                s = jnp.where(c <= r, s, NEG_BIG)
            m_prev = m_ref[h]
            m_new = jnp.maximum(m_prev, jnp.max(s, axis=-1, keepdims=True))
            p = jnp.exp(s - m_new)
            acc_ref[h] = jnp.exp(m_prev - m_new) * acc_ref[h] + _bdot(p.astype(BF16), va_ref[0, h])
            m_ref[h] = m_new

    @pl.when(ki < qi)
    def _():
        step(False)

    @pl.when(ki == qi)
    def _():
        step(True)
        for h in range(ATT_HEADS):
            a = acc_ref[h]
            o_ref[0, :, h * dh:(h + 1) * dh] = (a[:, :dh] / a[:, dh:dh + 1]).astype(BF16)


def _fox_attention(qa, ka, va):
    b, nh, l, _ = qa.shape
    t = min(ATT_TILE, l)
    nt = l // t
    pairs = [(q, k) for q in range(nt) for k in range(q + 1)]
    qi = jnp.asarray([p[0] for p in pairs], jnp.int32)
    ki = jnp.asarray([p[1] for p in pairs], jnp.int32)
    qspec = pl.BlockSpec((1, nh, t, HEAD_PAD), lambda bi, p, qi, ki: (bi, 0, qi[p], 0))
    kspec = pl.BlockSpec((1, nh, t, HEAD_PAD), lambda bi, p, qi, ki: (bi, 0, ki[p], 0))
    return pl.pallas_call(
        _fox_kernel,
        grid_spec=pltpu.PrefetchScalarGridSpec(
            num_scalar_prefetch=2,
            grid=(b, len(pairs)),
            in_specs=[qspec, kspec, kspec],
            out_specs=pl.BlockSpec((1, t, nh * ATT_HEAD_DIM), lambda bi, p, qi, ki: (bi, qi[p], 0)),
            scratch_shapes=[pltpu.VMEM((nh, t, 1), F32), pltpu.VMEM((nh, t, HEAD_PAD), F32)],
        ),
        out_shape=jax.ShapeDtypeStruct((b, l, nh * ATT_HEAD_DIM), BF16),
        compiler_params=pltpu.CompilerParams(dimension_semantics=("parallel", "arbitrary"),
                                             vmem_limit_bytes=VMEM_LIMIT),
        name="fox_attention",
    )(qi, ki, qa, ka, va)


def _s5_kernel(u_ref, m_ref, bc_ref, cc_ref, ar_ref, ai_ref, y_ref, v_ref, xp_ref):
    u = u_ref[...]
    rows = u.shape[0]
    half = ar_ref.shape[-1]
    v_ref[...] = _bdot(u, bc_ref[...])
    ar = ar_ref[...]
    ai = ai_ref[...]

    def group(i, carry):
        xr, xi = carry
        base = pl.multiple_of(i * 8, 8)
        blk = v_ref[pl.ds(base, 8), :]
        prev = []
        for j in range(8):
            prev.append(jnp.concatenate([xr, xi], axis=-1))
            vr = blk[j:j + 1, :half]
            vi = blk[j:j + 1, half:]
            xr, xi = ar * xr - ai * xi + vr, ar * xi + ai * xr + vi
        xp_ref[pl.ds(base, 8), :] = jnp.concatenate(prev, axis=0)
        return xr, xi

    zero = jnp.zeros((1, half), F32)
    lax.fori_loop(0, rows // 8, group, (zero, zero))
    y = _bdot(u, m_ref[...]) + _bdot(xp_ref[...].astype(BF16), cc_ref[...])
    y_ref[...] = y.astype(BF16)


def _s5(u2, m, bc, cc, ar, ai, b):
    rows_total, w = u2.shape
    rows = rows_total // b
    return pl.pallas_call(
        _s5_kernel,
        grid=(b,),
        in_specs=[pl.BlockSpec((rows, w), lambda i: (i, 0))] + [_resident(a.shape) for a in (m, bc, cc, ar, ai)],
        out_specs=pl.BlockSpec((rows, w), lambda i: (i, 0)),
        out_shape=jax.ShapeDtypeStruct((rows_total, w), BF16),
        scratch_shapes=[pltpu.VMEM((rows, bc.shape[1]), F32), pltpu.VMEM((rows, bc.shape[1]), F32)],
        compiler_params=pltpu.CompilerParams(dimension_semantics=("parallel",),
                                             vmem_limit_bytes=VMEM_LIMIT),
        name="s5",
    )(u2, m, bc, cc, ar, ai)


def _s5_operators(a_re, a_im, log_dt, b_re, b_im, c_re, c_im, d_skip):
    t = SSM_CHUNK
    g, p = a_re.shape
    hc = d_skip.shape[-1]
    lam = lax.complex(a_re.astype(F32), a_im.astype(F32))
    dt = jnp.exp(log_dt.astype(F32))[:, None]
    a_bar = jnp.exp(lam * dt)
    b_bar = ((a_bar - 1.0) / lam)[..., None] * lax.complex(b_re.astype(F32), b_im.astype(F32))
    c = lax.complex(c_re.astype(F32), c_im.astype(F32))
    tau = jnp.arange(t + 1, dtype=F32)
    apow = jnp.exp((lam * dt)[None] * tau[:, None, None])
    eye_g = jnp.eye(g, dtype=F32)
    kern = jnp.real(jnp.einsum('gop,tgp,gpi->tgio', c, apow[:t], b_bar))
    kern = kern.at[0].add(jnp.eye(hc, dtype=F32)[None] * d_skip.astype(F32)[:, :, None])
    s_idx = jnp.arange(t)[:, None]
    t_idx = jnp.arange(t)[None, :]
    diff = t_idx - s_idx
    kst = jnp.where((diff >= 0)[:, :, None, None, None], kern[jnp.clip(diff, 0, t - 1)], 0.0)
    m = jnp.einsum('stgio,gk->sgitko', kst, eye_g).reshape(t * g * hc, t * g * hc)
    bs = apow[:t][::-1][:, :, :, None] * b_bar[None]
    bcx = jnp.einsum('sgph,gk->sghkp', bs, eye_g.astype(bs.dtype)).reshape(t * g * hc, g * p)
    bc = jnp.concatenate([jnp.real(bcx), jnp.imag(bcx)], axis=1)
    ct = c[None] * apow[1:][:, :, None, :]
    ccx = jnp.einsum('tgop,gk->gptko', ct, eye_g.astype(ct.dtype)).reshape(g * p, t * g * hc)
    cc = jnp.concatenate([jnp.real(ccx), -jnp.imag(ccx)], axis=0)
    a_t = apow[t].reshape(1, g * p)
    return (m.astype(BF16), bc.astype(BF16), cc.astype(BF16),
            jnp.real(a_t).astype(F32), jnp.imag(a_t).astype(F32))


def _conv_kernel(h_ref, w_ref, b_ref, g_ref, beta_ref, o_ref, buf_ref):
    t = h_ref.shape[0]

    @pl.when(pl.program_id(1) == 0)
    def _():
        buf_ref[0:CONV_HALO, :] = jnp.zeros((CONV_HALO, buf_ref.shape[1]), F32)

    @pl.when(pl.program_id(1) > 0)
    def _():
        buf_ref[0:CONV_HALO, :] = buf_ref[t:t + CONV_HALO, :]

    buf_ref[CONV_HALO:, :] = h_ref[...].astype(F32)
    off = CONV_HALO - (CONV_K - 1)
    rb = min(CONV_ROWS, t)
    for r0 in range(0, t, rb):
        acc = jnp.zeros((rb, h_ref.shape[1]), F32)
        for j in range(CONV_K):
            acc = acc + w_ref[j:j + 1, :] * buf_ref[r0 + off + j:r0 + off + j + rb, :]
        y = acc + b_ref[...]
        mu = jnp.mean(y, axis=-1, keepdims=True)
        yc = y - mu
        var = jnp.mean(yc * yc, axis=-1, keepdims=True)
        z = yc * lax.rsqrt(var + EPS) * g_ref[...] + beta_ref[...]
        o_ref[r0:r0 + rb, :] = _silu(z).astype(BF16)


def _conv(hc, w, bias, g, beta, b, l):
    t = min(CONV_TILE, l)
    nt = l // t
    cw = hc.shape[1]
    return pl.pallas_call(
        _conv_kernel,
        grid=(b, nt),
        in_specs=[pl.BlockSpec((t, cw), lambda bi, i: (bi * nt + i, 0))]
                 + [_resident(a.shape) for a in (w, bias, g, beta)],
        out_specs=pl.BlockSpec((t, cw), lambda bi, i: (bi * nt + i, 0)),
        out_shape=jax.ShapeDtypeStruct(hc.shape, BF16),
        scratch_shapes=[pltpu.VMEM((t + CONV_HALO, cw), F32)],
        compiler_params=pltpu.CompilerParams(dimension_semantics=("parallel", "arbitrary"),
                                             vmem_limit_bytes=VMEM_LIMIT),
        name="conv",
    )(hc, w, bias, g, beta)


def _merge_kernel(x_ref, att_ref, y_ref, conv_ref, gates_ref, wglu_ref, wa_ref, ws_ref, wc_ref, wo_ref,
                  gf_ref, xo_ref, h_ref):
    d = x_ref.shape[-1]
    sw = y_ref.shape[-1]
    yg = _bdot(_gelu_tanh(y_ref[...].astype(F32)).astype(BF16), wglu_ref[...])
    ssm = (yg[:, :sw] * _sigmoid(yg[:, sw:])).astype(BF16)
    merged = gates_ref[:, 0:d].astype(F32) * _bdot(att_ref[...], wa_ref[...])
    merged = merged + gates_ref[:, d:2 * d].astype(F32) * _bdot(ssm, ws_ref[...])
    merged = merged + gates_ref[:, 2 * d:3 * d].astype(F32) * _bdot(conv_ref[...], wc_ref[...])
    xn = x_ref[...] + _bdot(merged.astype(BF16), wo_ref[...])
    xo_ref[...] = xn
    h_ref[...] = _rms(xn, gf_ref[...]).astype(BF16)


def _merge(x2, att, y, conv, gates, wglu, wa, ws, wc, wo, gf):
    n, d = x2.shape
    tm = min(TOKEN_TILE, n)
    row = lambda w: pl.BlockSpec((tm, w), lambda i: (i, 0))
    return pl.pallas_call(
        _merge_kernel,
        grid=(n // tm,),
        in_specs=[row(d), row(att.shape[1]), row(y.shape[1]), row(conv.shape[1]), row(gates.shape[1])]
                 + [_resident(a.shape) for a in (wglu, wa, ws, wc, wo, gf)],
        out_specs=[row(d), row(d)],
        out_shape=[jax.ShapeDtypeStruct((n, d), F32), jax.ShapeDtypeStruct((n, d), BF16)],
        compiler_params=pltpu.CompilerParams(dimension_semantics=("parallel",),
                                             vmem_limit_bytes=VMEM_LIMIT),
        name="merge",
    )(x2, att, y, conv, gates, wglu, wa, ws, wc, wo, gf)


def _ffn_kernel(x_ref, h_ref, wg_ref, wu_ref, wd_ref, xo_ref, *, chunks):
    h = h_ref[...]
    f = wg_ref.shape[1]
    fc = f // chunks
    acc = x_ref[...]
    for j in range(chunks):
        sl = slice(j * fc, (j + 1) * fc)
        act = (_silu(_bdot(h, wg_ref[:, sl])) * _bdot(h, wu_ref[:, sl])).astype(BF16)
        acc = acc + _bdot(act, wd_ref[sl, :])
    xo_ref[...] = acc


def _ffn(x2, h, wg, wu, wd):
    n, d = x2.shape
    tm = min(TOKEN_TILE, n)
    row = pl.BlockSpec((tm, d), lambda i: (i, 0))
    f = wg.shape[1]
    chunks = 2 if f % (2 * LANES) == 0 else 1
    return pl.pallas_call(
        functools.partial(_ffn_kernel, chunks=chunks),
        grid=(n // tm,),
        in_specs=[row, row] + [_resident(a.shape) for a in (wg, wu, wd)],
        out_specs=row,
        out_shape=jax.ShapeDtypeStruct((n, d), F32),
        compiler_params=pltpu.CompilerParams(dimension_semantics=("parallel",),
                                             vmem_limit_bytes=VMEM_LIMIT),
        name="ffn",
    )(x2, h, wg, wu, wd)


def _router_kernel(x_ref, g_ref, wr_ref, br_ref, w_ref):
    h = _rms(x_ref[...], g_ref[...])
    logits = jnp.dot(h, wr_ref[...], preferred_element_type=F32, precision=HIGHEST) + br_ref[...]
    lane = lax.broadcasted_iota(jnp.int32, logits.shape, 1)
    logits = jnp.where(lane < N_EXPERTS, logits, NEG_BIG)
    m1 = jnp.max(logits, axis=-1, keepdims=True)
    i1 = jnp.min(jnp.where(logits == m1, lane, LANES), axis=-1, keepdims=True)
    rest = jnp.where(lane == i1, NEG_BIG, logits)
    m2 = jnp.max(rest, axis=-1, keepdims=True)
    i2 = jnp.min(jnp.where(rest == m2, lane, LANES), axis=-1, keepdims=True)
    e = jnp.exp(m2 - m1)
    p1 = 1.0 / (1.0 + e)
    p2 = e / (1.0 + e)
    w_ref[...] = jnp.where(lane == i1, p1, 0.0) + jnp.where(lane == i2, p2, 0.0)


def _router(x2, g, wr, br):
    n, d = x2.shape
    tm = min(TOKEN_TILE, n)
    return pl.pallas_call(
        _router_kernel,
        grid=(n // tm,),
        in_specs=[pl.BlockSpec((tm, d), lambda i: (i, 0))] + [_resident(a.shape) for a in (g, wr, br)],
        out_specs=pl.BlockSpec((tm, LANES), lambda i: (i, 0)),
        out_shape=jax.ShapeDtypeStruct((n, LANES), F32),
        compiler_params=pltpu.CompilerParams(dimension_semantics=("parallel",),
                                             vmem_limit_bytes=VMEM_LIMIT),
        name="router",
    )(x2, g, wr, br)


def _moe_kernel(x_ref, h_ref, w_ref, wg_ref, wu_ref, wd_ref, xo_ref):
    e = pl.program_id(1)

    @pl.when((e == 0) & (pl.program_id(2) == 0))
    def _():
        xo_ref[...] = x_ref[...]

    h = h_ref[...]
    act = (_silu(_bdot(h, wg_ref[0])) * _bdot(h, wu_ref[0])).astype(BF16)
    lane = lax.broadcasted_iota(jnp.int32, w_ref.shape, 1)
    scale = jnp.sum(jnp.where(lane == e, w_ref[...], 0.0), axis=-1, keepdims=True)
    xo_ref[...] += scale * _bdot(act, wd_ref[0])


def _moe(x2, h, wts, wg, wu, wd):
    n, d = x2.shape
    tm = min(MOE_TILE, n)
    ne, _, f = wg.shape
    halves = 2
    fh = f // halves
    row = lambda w: pl.BlockSpec((tm, w), lambda i, e, j: (i, 0))
    return pl.pallas_call(
        _moe_kernel,
        grid=(n // tm, ne, halves),
        in_specs=[row(d), row(d), row(LANES),
                  pl.BlockSpec((1, d, fh), lambda i, e, j: (e, 0, j)),
                  pl.BlockSpec((1, d, fh), lambda i, e, j: (e, 0, j)),
                  pl.BlockSpec((1, fh, d), lambda i, e, j: (e, j, 0))],
        out_specs=row(d),
        out_shape=jax.ShapeDtypeStruct((n, d), F32),
        compiler_params=pltpu.CompilerParams(dimension_semantics=("parallel", "arbitrary", "arbitrary"),
                                             vmem_limit_bytes=VMEM_LIMIT),
        name="moe",
    )(x2, h, wts, wg, wu, wd)


def _final_kernel(x_ref, g_ref, o_ref):
    o_ref[...] = _rms(x_ref[...], g_ref[...])


def _final_norm(x2, g):
    n, d = x2.shape
    tm = min(TOKEN_TILE, n)
    row = pl.BlockSpec((tm, d), lambda i: (i, 0))
    return pl.pallas_call(
        _final_kernel,
        grid=(n // tm,),
        in_specs=[row, _resident(g.shape)],
        out_specs=row,
        out_shape=jax.ShapeDtypeStruct((n, d), F32),
        compiler_params=pltpu.CompilerParams(dimension_semantics=("parallel",),
                                             vmem_limit_bytes=VMEM_LIMIT),
        name="final_norm",
    )(x2, g)


def _pad_cols(a, width):
    return jnp.pad(a, ((0, 0), (0, width - a.shape[1])))


def kernel(x, g_mix, w_in, b_in, ssm_a_re, ssm_a_im, ssm_log_dt, ssm_b_re, ssm_b_im, ssm_c_re, ssm_c_im, ssm_d, ssm_w_glu, conv_w, conv_b, conv_ln_g, conv_ln_b, w_att_out, w_ssm_out, w_conv_out, w_o, g_ffn, ffn_w_gate, ffn_w_up, ffn_w_down, moe_w_router, moe_b_router, moe_w_gate, moe_w_up, moe_w_down, g_final):
    b, l, d = x.shape
    depth = w_in.shape[0]
    n = b * l
    aw = ATT_HEADS * ATT_HEAD_DIM
    sw = SSM_GROUPS * SSM_GROUP
    cw = conv_w.shape[-1]
    q_end, k_end, v_end = aw, 2 * aw, 3 * aw
    f_end = v_end + ATT_HEADS
    u_end = f_end + sw
    c_end = u_end + 2 * cw
    scale = ATT_HEAD_DIM ** -0.5
    col_scale = jnp.concatenate([jnp.full((aw,), scale, F32), jnp.ones((2 * aw,), F32)])
    row2 = lambda v: v.reshape(1, -1).astype(F32)

    x2 = x.reshape(n, d)
    for layer in range(depth):
        w = w_in[layer]
        bias = b_in[layer]
        wqkv = (w[:, :v_end] * col_scale).astype(BF16)
        bqkv = row2(bias[:v_end] * col_scale)
        wf = _pad_cols(w[:, v_end:f_end], LANES).astype(BF16)
        bf = _pad_cols(row2(bias[v_end:f_end]), LANES)
        qkv, logf, u, hc, gates = _inproj(
            x2, row2(g_mix[layer]), wqkv, bqkv, wf, bf,
            w[:, f_end:u_end].astype(BF16), row2(bias[f_end:u_end]),
            w[:, u_end:c_end].astype(BF16), row2(bias[u_end:c_end]),
            w[:, c_end:].astype(BF16), row2(bias[c_end:]))

        qa, ka, va = _foxprep(qkv, logf, b, l)
        att = _fox_attention(qa, ka, va).reshape(n, aw)

        m, bc, cc, ar, ai = _s5_operators(ssm_a_re[layer], ssm_a_im[layer], ssm_log_dt[layer],
                                          ssm_b_re[layer], ssm_b_im[layer], ssm_c_re[layer],
                                          ssm_c_im[layer], ssm_d[layer])
        y = _s5(u.reshape(n // SSM_CHUNK, SSM_CHUNK * sw), m, bc, cc, ar, ai, b).reshape(n, sw)

        wconv = jnp.pad(conv_w[layer].astype(F32), ((0, CONV_HALO - CONV_K), (0, 0)))
        conv = _conv(hc, wconv, row2(conv_b[layer]), row2(conv_ln_g[layer]), row2(conv_ln_b[layer]), b, l)

        x2, h = _merge(x2, att, y, conv, gates, ssm_w_glu[layer].astype(BF16),
                       w_att_out[layer].astype(BF16), w_ssm_out[layer].astype(BF16),
                       w_conv_out[layer].astype(BF16), w_o[layer].astype(BF16), row2(g_ffn[layer]))

        i = layer // 2
        if layer % 2 == 0:
            x2 = _ffn(x2, h, ffn_w_gate[i].astype(BF16), ffn_w_up[i].astype(BF16), ffn_w_down[i].astype(BF16))
        else:
            wts = _router(x2, row2(g_ffn[layer]), _pad_cols(moe_w_router[i].astype(F32), LANES),
                          _pad_cols(row2(moe_b_router[i]), LANES))
            x2 = _moe(x2, h, wts, moe_w_gate[i].astype(BF16), moe_w_up[i].astype(BF16),
                      moe_w_down[i].astype(BF16))
    return _final_norm(x2, row2(g_final)).reshape(b, l, d)
```

```python
import functools
import math

import jax
import jax.numpy as jnp
from jax import lax
from jax.experimental import pallas as pl
from jax.experimental.pallas import tpu as pltpu

F32 = jnp.float32
BF16 = jnp.bfloat16
HIGHEST = lax.Precision.HIGHEST

EPS = 1e-6
ATT_HEADS = 8
ATT_HEAD_DIM = 64
SSM_GROUPS = 16
SSM_GROUP = 16
SSM_STATE = 64
CONV_K = 31
N_EXPERTS = 8
LANES = 128
HEAD_PAD = 128
SSM_CHUNK = 8
CONV_HALO = 32
NEG_BIG = -1e30

TOKEN_TILE = 512
ATT_TILE = 512
CONV_TILE = 512
CONV_ROWS = 64
MOE_TILE = 512
VMEM_LIMIT = 56 * 1024 * 1024


def _resident(shape):
    nd = len(shape)
    return pl.BlockSpec(shape, lambda *_: (0,) * nd, pipeline_mode=pl.Buffered(1))


def _sigmoid(x):
    return 1.0 / (1.0 + jnp.exp(-x))


def _silu(x):
    return x * _sigmoid(x)


def _gelu_tanh(x):
    c = math.sqrt(2.0 / math.pi)
    return 0.5 * x * (1.0 + jnp.tanh(c * (x + 0.044715 * (x * x * x))))


def _rms(x, g):
    return x * lax.rsqrt(jnp.mean(x * x, axis=-1, keepdims=True) + EPS) * g


def _bdot(a, b):
    return jnp.dot(a, b, preferred_element_type=F32)


def _inproj_kernel(x_ref, g_ref, wqkv_ref, bqkv_ref, wf_ref, bf_ref, wu_ref, bu_ref,
                   wc_ref, bc_ref, wg_ref, bg_ref,
                   qkv_ref, logf_ref, u_ref, hc_ref, gates_ref):
    h = _rms(x_ref[...], g_ref[...]).astype(BF16)
    qkv_ref[...] = (_bdot(h, wqkv_ref[...]) + bqkv_ref[...]).astype(BF16)
    fz = _bdot(h, wf_ref[...]) + bf_ref[...]
    logf_ref[...] = jnp.minimum(fz, 0.0) - jnp.log1p(jnp.exp(-jnp.abs(fz)))
    u_ref[...] = (_bdot(h, wu_ref[...]) + bu_ref[...]).astype(BF16)
    cz = _bdot(h, wc_ref[...]) + bc_ref[...]
    cw = cz.shape[-1] // 2
    hc_ref[...] = (cz[:, :cw] * _sigmoid(cz[:, cw:])).astype(BF16)
    d = x_ref.shape[-1]
    for j in range(gates_ref.shape[-1] // d):
        gz = _bdot(h, wg_ref[:, j * d:(j + 1) * d]) + bg_ref[:, j * d:(j + 1) * d]
        gates_ref[:, j * d:(j + 1) * d] = _sigmoid(gz).astype(BF16)


def _inproj(x2, g, wqkv, bqkv, wf, bf, wu, bu, wc, bc, wg, bg):
    n, d = x2.shape
    tm = min(TOKEN_TILE, n)
    row = lambda w: pl.BlockSpec((tm, w), lambda i: (i, 0))
    outs = [(wqkv.shape[1], BF16), (LANES, F32), (wu.shape[1], BF16), (wc.shape[1] // 2, BF16),
            (wg.shape[1], BF16)]
    return pl.pallas_call(
        _inproj_kernel,
        grid=(n // tm,),
        in_specs=[row(d)] + [_resident(a.shape) for a in (g, wqkv, bqkv, wf, bf, wu, bu, wc, bc, wg, bg)],
        out_specs=[row(w) for w, _ in outs],
        out_shape=[jax.ShapeDtypeStruct((n, w), dt) for w, dt in outs],
        compiler_params=pltpu.CompilerParams(dimension_semantics=("parallel",),
                                             vmem_limit_bytes=VMEM_LIMIT),
        name="inproj",
    )(x2, g, wqkv, bqkv, wf, bf, wu, bu, wc, bc, wg, bg)


def _split3(f):
    hi = f.astype(BF16).astype(F32)
    r = f - hi
    mid = r.astype(BF16).astype(F32)
    lo = (r - mid).astype(BF16).astype(F32)
    return hi, mid, lo


def _foxprep_kernel(qkv_ref, logf_ref, qa_ref, ka_ref, vt_ref, carry_ref):
    t = qkv_ref.shape[0]

    @pl.when(pl.program_id(1) == 0)
    def _():
        carry_ref[...] = jnp.zeros_like(carry_ref)

    r = lax.broadcasted_iota(jnp.int32, (t, t), 0)
    c = lax.broadcasted_iota(jnp.int32, (t, t), 1)
    tri = (c <= r).astype(F32)
    fcum = jnp.dot(tri, logf_ref[...], preferred_element_type=F32, precision=HIGHEST) + carry_ref[...]
    carry_ref[...] = fcum[t - 1:t, :]

    dh = ATT_HEAD_DIM
    aw = ATT_HEADS * dh
    lane = lax.broadcasted_iota(jnp.int32, (t, HEAD_PAD - dh), 1)
    for h in range(ATT_HEADS):
        hi, mid, lo = _split3(fcum[:, h:h + 1])
        one = jnp.where(lane < 6, 1.0, 0.0)
        qx = jnp.where(lane == 0, hi, jnp.where(lane == 1, mid, jnp.where(lane == 2, lo, one)))
        kx = jnp.where(lane == 3, -hi, jnp.where(lane == 4, -mid, jnp.where(lane == 5, -lo, one)))
        vx = jnp.where(lane == 0, 1.0, 0.0)
        q = qkv_ref[:, h * dh:(h + 1) * dh].astype(F32)
        k = qkv_ref[:, aw + h * dh:aw + (h + 1) * dh].astype(F32)
        v = qkv_ref[:, 2 * aw + h * dh:2 * aw + (h + 1) * dh].astype(F32)
        qa_ref[0, h] = jnp.concatenate([q, qx], axis=-1).astype(BF16)
        ka_ref[0, h] = jnp.concatenate([k, kx], axis=-1).astype(BF16)
        vt_ref[0, h] = jnp.concatenate([v, vx], axis=-1).T.astype(BF16)


def _foxprep(qkv, logf, b, l):
    t = min(ATT_TILE, l)
    nt = l // t
    out = jax.ShapeDtypeStruct((b, ATT_HEADS, l, HEAD_PAD), BF16)
    out_t = jax.ShapeDtypeStruct((b, ATT_HEADS, HEAD_PAD, l), BF16)
    ospec = pl.BlockSpec((1, ATT_HEADS, t, HEAD_PAD), lambda bi, i: (bi, 0, i, 0))
    ospec_t = pl.BlockSpec((1, ATT_HEADS, HEAD_PAD, t), lambda bi, i: (bi, 0, 0, i))
    return pl.pallas_call(
        _foxprep_kernel,
        grid=(b, nt),
        in_specs=[pl.BlockSpec((t, qkv.shape[1]), lambda bi, i: (bi * nt + i, 0)),
                  pl.BlockSpec((t, LANES), lambda bi, i: (bi * nt + i, 0))],
        out_specs=[ospec, ospec, ospec_t],
        out_shape=[out, out, out_t],
        scratch_shapes=[pltpu.VMEM((1, LANES), F32)],
        compiler_params=pltpu.CompilerParams(dimension_semantics=("parallel", "arbitrary"),
                                             vmem_limit_bytes=VMEM_LIMIT),
        name="foxprep",
    )(qkv, logf)


def _fox_kernel(qi_ref, ki_ref, qa_ref, ka_ref, vt_ref, o_ref, m_ref, acc_ref):
    p_id = pl.program_id(1)
    qi = qi_ref[p_id]
    ki = ki_ref[p_id]
    t = qa_ref.shape[2]
    dh = ATT_HEAD_DIM

    @pl.when(ki == 0)
    def _():
        m_ref[...] = jnp.full_like(m_ref, NEG_BIG)
        acc_ref[...] = jnp.zeros_like(acc_ref)

    def step(masked):
        for h in range(ATT_HEADS):
            st = lax.dot_general(ka_ref[0, h], qa_ref[0, h], (((1,), (1,)), ((), ())),
                                 preferred_element_type=F32)
            if masked:
                kpos = lax.broadcasted_iota(jnp.int32, (t, t), 0)
                qpos = lax.broadcasted_iota(jnp.int32, (t, t), 1)
                st = jnp.where(kpos <= qpos, st, NEG_BIG)
            m_prev = m_ref[h]
            m_new = jnp.maximum(m_prev, jnp.max(st, axis=0, keepdims=True))
            p = jnp.exp(st - m_new).astype(BF16)
            acc_ref[h] = jnp.exp(m_prev - m_new) * acc_ref[h] + _bdot(vt_ref[0, h], p)
            m_ref[h] = m_new

    @pl.when(ki < qi)
    def _():
        step(False)

    @pl.when(ki == qi)
    def _():
        step(True)
        for h in range(ATT_HEADS):
            a = acc_ref[h]
            o = (a / a[dh:dh + 1, :]).T
            o_ref[0, :, h * dh:(h + 1) * dh] = o[:, :dh].astype(BF16)


def _fox_attention(qa, ka, vt):
    b, nh, l, _ = qa.shape
    t = min(ATT_TILE, l)
    nt = l // t
    pairs = [(q, k) for q in range(nt) for k in range(q + 1)]
    qi = jnp.asarray([p[0] for p in pairs], jnp.int32)
    ki = jnp.asarray([p[1] for p in pairs], jnp.int32)
    qspec = pl.BlockSpec((1, nh, t, HEAD_PAD), lambda bi, p, qi, ki: (bi, 0, qi[p], 0))
    kspec = pl.BlockSpec((1, nh, t, HEAD_PAD), lambda bi, p, qi, ki: (bi, 0, ki[p], 0))
    vspec = pl.BlockSpec((1, nh, HEAD_PAD, t), lambda bi, p, qi, ki: (bi, 0, 0, ki[p]))
    return pl.pallas_call(
        _fox_kernel,
        grid_spec=pltpu.PrefetchScalarGridSpec(
            num_scalar_prefetch=2,
            grid=(b, len(pairs)),
            in_specs=[qspec, kspec, vspec],
            out_specs=pl.BlockSpec((1, t, nh * ATT_HEAD_DIM), lambda bi, p, qi, ki: (bi, qi[p], 0)),
            scratch_shapes=[pltpu.VMEM((nh, 1, t), F32), pltpu.VMEM((nh, HEAD_PAD, t), F32)],
        ),
        out_shape=jax.ShapeDtypeStruct((b, l, nh * ATT_HEAD_DIM), BF16),
        compiler_params=pltpu.CompilerParams(dimension_semantics=("parallel", "arbitrary"),
                                             vmem_limit_bytes=VMEM_LIMIT),
        name="fox_attention",
    )(qi, ki, qa, ka, vt)


def _s5_kernel(u_ref, m_ref, bc_ref, cc_ref, ar_ref, ai_ref, y_ref, v_ref, xp_ref):
    u = u_ref[...]
    rows = u.shape[0]
    half = ar_ref.shape[-1]
    v_ref[...] = _bdot(u, bc_ref[...])
    ar = ar_ref[...]
    ai = ai_ref[...]

    def group(i, carry):
        xr, xi = carry
        base = pl.multiple_of(i * 8, 8)
        blk = v_ref[pl.ds(base, 8), :]
        prev = []
        for j in range(8):
            prev.append(jnp.concatenate([xr, xi], axis=-1))
            vr = blk[j:j + 1, :half]
            vi = blk[j:j + 1, half:]
            xr, xi = ar * xr - ai * xi + vr, ar * xi + ai * xr + vi
        xp_ref[pl.ds(base, 8), :] = jnp.concatenate(prev, axis=0)
        return xr, xi

    zero = jnp.zeros((1, half), F32)
    lax.fori_loop(0, rows // 8, group, (zero, zero))
    y = _bdot(u, m_ref[...]) + _bdot(xp_ref[...].astype(BF16), cc_ref[...])
    y_ref[...] = y.astype(BF16)


def _s5(u2, m, bc, cc, ar, ai, b):
    rows_total, w = u2.shape
    rows = rows_total // b
    return pl.pallas_call(
        _s5_kernel,
        grid=(b,),
        in_specs=[pl.BlockSpec((rows, w), lambda i: (i, 0))] + [_resident(a.shape) for a in (m, bc, cc, ar, ai)],
        out_specs=pl.BlockSpec((rows, w), lambda i: (i, 0)),
        out_shape=jax.ShapeDtypeStruct((rows_total, w), BF16),
        scratch_shapes=[pltpu.VMEM((rows, bc.shape[1]), F32), pltpu.VMEM((rows, bc.shape[1]), F32)],
        compiler_params=pltpu.CompilerParams(dimension_semantics=("parallel",),
                                             vmem_limit_bytes=VMEM_LIMIT),
        name="s5",
    )(u2, m, bc, cc, ar, ai)


def _s5_operators(a_re, a_im, log_dt, b_re, b_im, c_re, c_im, d_skip):
    t = SSM_CHUNK
    g, p = a_re.shape
    hc = d_skip.shape[-1]
    lam = lax.complex(a_re.astype(F32), a_im.astype(F32))
    dt = jnp.exp(log_dt.astype(F32))[:, None]
    a_bar = jnp.exp(lam * dt)
    b_bar = ((a_bar - 1.0) / lam)[..., None] * lax.complex(b_re.astype(F32), b_im.astype(F32))
    c = lax.complex(c_re.astype(F32), c_im.astype(F32))
    tau = jnp.arange(t + 1, dtype=F32)
    apow = jnp.exp((lam * dt)[None] * tau[:, None, None])
    eye_g = jnp.eye(g, dtype=F32)
    kern = jnp.real(jnp.einsum('gop,tgp,gpi->tgio', c, apow[:t], b_bar))
    kern = kern.at[0].add(jnp.eye(hc, dtype=F32)[None] * d_skip.astype(F32)[:, :, None])
    s_idx = jnp.arange(t)[:, None]
    t_idx = jnp.arange(t)[None, :]
    diff = t_idx - s_idx
    kst = jnp.where((diff >= 0)[:, :, None, None, None], kern[jnp.clip(diff, 0, t - 1)], 0.0)
    m = jnp.einsum('stgio,gk->sgitko', kst, eye_g).reshape(t * g * hc, t * g * hc)
    bs = apow[:t][::-1][:, :, :, None] * b_bar[None]
    bcx = jnp.einsum('sgph,gk->sghkp', bs, eye_g.astype(bs.dtype)).reshape(t * g * hc, g * p)
    bc = jnp.concatenate([jnp.real(bcx), jnp.imag(bcx)], axis=1)
    ct = c[None] * apow[1:][:, :, None, :]
    ccx = jnp.einsum('tgop,gk->gptko', ct, eye_g.astype(ct.dtype)).reshape(g * p, t * g * hc)
    cc = jnp.concatenate([jnp.real(ccx), -jnp.imag(ccx)], axis=0)
    a_t = apow[t].reshape(1, g * p)
    return (m.astype(BF16), bc.astype(BF16), cc.astype(BF16),
            jnp.real(a_t).astype(F32), jnp.imag(a_t).astype(F32))


def _conv_kernel(h_ref, w_ref, b_ref, g_ref, beta_ref, o_ref, buf_ref):
    t = h_ref.shape[0]

    @pl.when(pl.program_id(1) == 0)
    def _():
        buf_ref[0:CONV_HALO, :] = jnp.zeros((CONV_HALO, buf_ref.shape[1]), F32)

    @pl.when(pl.program_id(1) > 0)
    def _():
        buf_ref[0:CONV_HALO, :] = buf_ref[t:t + CONV_HALO, :]

    buf_ref[CONV_HALO:, :] = h_ref[...].astype(F32)
    off = CONV_HALO - (CONV_K - 1)
    rb = min(CONV_ROWS, t)
    for r0 in range(0, t, rb):
        acc = jnp.zeros((rb, h_ref.shape[1]), F32)
        for j in range(CONV_K):
            acc = acc + w_ref[j:j + 1, :] * buf_ref[r0 + off + j:r0 + off + j + rb, :]
        y = acc + b_ref[...]
        mu = jnp.mean(y, axis=-1, keepdims=True)
        yc = y - mu
        var = jnp.mean(yc * yc, axis=-1, keepdims=True)
        z = yc * lax.rsqrt(var + EPS) * g_ref[...] + beta_ref[...]
        o_ref[r0:r0 + rb, :] = _silu(z).astype(BF16)


def _conv(hc, w, bias, g, beta, b, l):
    t = min(CONV_TILE, l)
    nt = l // t
    cw = hc.shape[1]
    return pl.pallas_call(
        _conv_kernel,
        grid=(b, nt),
        in_specs=[pl.BlockSpec((t, cw), lambda bi, i: (bi * nt + i, 0))]
                 + [_resident(a.shape) for a in (w, bias, g, beta)],
        out_specs=pl.BlockSpec((t, cw), lambda bi, i: (bi * nt + i, 0)),
        out_shape=jax.ShapeDtypeStruct(hc.shape, BF16),
        scratch_shapes=[pltpu.VMEM((t + CONV_HALO, cw), F32)],
        compiler_params=pltpu.CompilerParams(dimension_semantics=("parallel", "arbitrary"),
                                             vmem_limit_bytes=VMEM_LIMIT),
        name="conv",
    )(hc, w, bias, g, beta)


def _merge_kernel(x_ref, att_ref, y_ref, conv_ref, gates_ref, wglu_ref, wa_ref, ws_ref, wc_ref, wo_ref,
                  gf_ref, xo_ref, h_ref):
    d = x_ref.shape[-1]
    sw = y_ref.shape[-1]
    yg = _bdot(_gelu_tanh(y_ref[...].astype(F32)).astype(BF16), wglu_ref[...])
    ssm = (yg[:, :sw] * _sigmoid(yg[:, sw:])).astype(BF16)
    merged = gates_ref[:, 0:d].astype(F32) * _bdot(att_ref[...], wa_ref[...])
    merged = merged + gates_ref[:, d:2 * d].astype(F32) * _bdot(ssm, ws_ref[...])
    merged = merged + gates_ref[:, 2 * d:3 * d].astype(F32) * _bdot(conv_ref[...], wc_ref[...])
    xn = x_ref[...] + _bdot(merged.astype(BF16), wo_ref[...])
    xo_ref[...] = xn
    h_ref[...] = _rms(xn, gf_ref[...]).astype(BF16)


def _merge(x2, att, y, conv, gates, wglu, wa, ws, wc, wo, gf):
    n, d = x2.shape
    tm = min(TOKEN_TILE, n)
    row = lambda w: pl.BlockSpec((tm, w), lambda i: (i, 0))
    return pl.pallas_call(
        _merge_kernel,
        grid=(n // tm,),
        in_specs=[row(d), row(att.shape[1]), row(y.shape[1]), row(conv.shape[1]), row(gates.shape[1])]
                 + [_resident(a.shape) for a in (wglu, wa, ws, wc, wo, gf)],
        out_specs=[row(d), row(d)],
        out_shape=[jax.ShapeDtypeStruct((n, d), F32), jax.ShapeDtypeStruct((n, d), BF16)],
        compiler_params=pltpu.CompilerParams(dimension_semantics=("parallel",),
                                             vmem_limit_bytes=VMEM_LIMIT),
        name="merge",
    )(x2, att, y, conv, gates, wglu, wa, ws, wc, wo, gf)


def _ffn_kernel(x_ref, h_ref, wg_ref, wu_ref, wd_ref, xo_ref, *, chunks):
    h = h_ref[...]
    f = wg_ref.shape[1]
    fc = f // chunks
    acc = x_ref[...]
    for j in range(chunks):
        sl = slice(j * fc, (j + 1) * fc)
        act = (_silu(_bdot(h, wg_ref[:, sl])) * _bdot(h, wu_ref[:, sl])).astype(BF16)
        acc = acc + _bdot(act, wd_ref[sl, :])
    xo_ref[...] = acc


def _ffn(x2, h, wg, wu, wd):
    n, d = x2.shape
    tm = min(TOKEN_TILE, n)
    row = pl.BlockSpec((tm, d), lambda i: (i, 0))
    f = wg.shape[1]
    chunks = 2 if f % (2 * LANES) == 0 else 1
    return pl.pallas_call(
        functools.partial(_ffn_kernel, chunks=chunks),
        grid=(n // tm,),
        in_specs=[row, row] + [_resident(a.shape) for a in (wg, wu, wd)],
        out_specs=row,
        out_shape=jax.ShapeDtypeStruct((n, d), F32),
        compiler_params=pltpu.CompilerParams(dimension_semantics=("parallel",),
                                             vmem_limit_bytes=VMEM_LIMIT),
        name="ffn",
    )(x2, h, wg, wu, wd)


def _router_kernel(x_ref, g_ref, wr_ref, br_ref, w_ref):
    h = _rms(x_ref[...], g_ref[...])
    logits = jnp.dot(h, wr_ref[...], preferred_element_type=F32, precision=HIGHEST) + br_ref[...]
    lane = lax.broadcasted_iota(jnp.int32, logits.shape, 1)
    logits = jnp.where(lane < N_EXPERTS, logits, NEG_BIG)
    m1 = jnp.max(logits, axis=-1, keepdims=True)
    i1 = jnp.min(jnp.where(logits == m1, lane, LANES), axis=-1, keepdims=True)
    rest = jnp.where(lane == i1, NEG_BIG, logits)
    m2 = jnp.max(rest, axis=-1, keepdims=True)
    i2 = jnp.min(jnp.where(rest == m2, lane, LANES), axis=-1, keepdims=True)
    e = jnp.exp(m2 - m1)
    p1 = 1.0 / (1.0 + e)
    p2 = e / (1.0 + e)
    w_ref[...] = jnp.where(lane == i1, p1, 0.0) + jnp.where(lane == i2, p2, 0.0)


def _router(x2, g, wr, br):
    n, d = x2.shape
    tm = min(TOKEN_TILE, n)
    return pl.pallas_call(
        _router_kernel,
        grid=(n // tm,),
        in_specs=[pl.BlockSpec((tm, d), lambda i: (i, 0))] + [_resident(a.shape) for a in (g, wr, br)],
        out_specs=pl.BlockSpec((tm, LANES), lambda i: (i, 0)),
        out_shape=jax.ShapeDtypeStruct((n, LANES), F32),
        compiler_params=pltpu.CompilerParams(dimension_semantics=("parallel",),
                                             vmem_limit_bytes=VMEM_LIMIT),
        name="router",
    )(x2, g, wr, br)


def _moe_kernel(x_ref, h_ref, w_ref, wg_ref, wu_ref, wd_ref, xo_ref):
    e = pl.program_id(1)

    @pl.when((e == 0) & (pl.program_id(2) == 0))
    def _():
        xo_ref[...] = x_ref[...]

    h = h_ref[...]
    act = (_silu(_bdot(h, wg_ref[0])) * _bdot(h, wu_ref[0])).astype(BF16)
    lane = lax.broadcasted_iota(jnp.int32, w_ref.shape, 1)
    scale = jnp.sum(jnp.where(lane == e, w_ref[...], 0.0), axis=-1, keepdims=True)
    xo_ref[...] += scale * _bdot(act, wd_ref[0])


def _moe(x2, h, wts, wg, wu, wd):
    n, d = x2.shape
    tm = min(MOE_TILE, n)
    ne, _, f = wg.shape
    halves = 2
    fh = f // halves
    row = lambda w: pl.BlockSpec((tm, w), lambda i, e, j: (i, 0))
    return pl.pallas_call(
        _moe_kernel,
        grid=(n // tm, ne, halves),
        in_specs=[row(d), row(d), row(LANES),
                  pl.BlockSpec((1, d, fh), lambda i, e, j: (e, 0, j)),
                  pl.BlockSpec((1, d, fh), lambda i, e, j: (e, 0, j)),
                  pl.BlockSpec((1, fh, d), lambda i, e, j: (e, j, 0))],
        out_specs=row(d),
        out_shape=jax.ShapeDtypeStruct((n, d), F32),
        compiler_params=pltpu.CompilerParams(dimension_semantics=("parallel", "arbitrary", "arbitrary"),
                                             vmem_limit_bytes=VMEM_LIMIT),
        name="moe",
    )(x2, h, wts, wg, wu, wd)


def _final_kernel(x_ref, g_ref, o_ref):
    o_ref[...] = _rms(x_ref[...], g_ref[...])


def _final_norm(x2, g):
    n, d = x2.shape
    tm = min(TOKEN_TILE, n)
    row = pl.BlockSpec((tm, d), lambda i: (i, 0))
    return pl.pallas_call(
        _final_kernel,
        grid=(n // tm,),
        in_specs=[row, _resident(g.shape)],
        out_specs=row,
        out_shape=jax.ShapeDtypeStruct((n, d), F32),
        compiler_params=pltpu.CompilerParams(dimension_semantics=("parallel",),
                                             vmem_limit_bytes=VMEM_LIMIT),
        name="final_norm",
    )(x2, g)


def _pad_cols(a, width):
    return jnp.pad(a, ((0, 0), (0, width - a.shape[1])))


def kernel(x, g_mix, w_in, b_in, ssm_a_re, ssm_a_im, ssm_log_dt, ssm_b_re, ssm_b_im, ssm_c_re, ssm_c_im, ssm_d, ssm_w_glu, conv_w, conv_b, conv_ln_g, conv_ln_b, w_att_out, w_ssm_out, w_conv_out, w_o, g_ffn, ffn_w_gate, ffn_w_up, ffn_w_down, moe_w_router, moe_b_router, moe_w_gate, moe_w_up, moe_w_down, g_final):
    b, l, d = x.shape
    depth = w_in.shape[0]
    n = b * l
    aw = ATT_HEADS * ATT_HEAD_DIM
    sw = SSM_GROUPS * SSM_GROUP
    cw = conv_w.shape[-1]
    q_end, k_end, v_end = aw, 2 * aw, 3 * aw
    f_end = v_end + ATT_HEADS
    u_end = f_end + sw
    c_end = u_end + 2 * cw
    scale = ATT_HEAD_DIM ** -0.5
    col_scale = jnp.concatenate([jnp.full((aw,), scale, F32), jnp.ones((2 * aw,), F32)])
    row2 = lambda v: v.reshape(1, -1).astype(F32)

    x2 = x.reshape(n, d)
    for layer in range(depth):
        w = w_in[layer]
        bias = b_in[layer]
        wqkv = (w[:, :v_end] * col_scale).astype(BF16)
        bqkv = row2(bias[:v_end] * col_scale)
        wf = _pad_cols(w[:, v_end:f_end], LANES).astype(BF16)
        bf = _pad_cols(row2(bias[v_end:f_end]), LANES)
        qkv, logf, u, hc, gates = _inproj(
            x2, row2(g_mix[layer]), wqkv, bqkv, wf, bf,
            w[:, f_end:u_end].astype(BF16), row2(bias[f_end:u_end]),
            w[:, u_end:c_end].astype(BF16), row2(bias[u_end:c_end]),
            w[:, c_end:].astype(BF16), row2(bias[c_end:]))

        qa, ka, va = _foxprep(qkv, logf, b, l)
        att = _fox_attention(qa, ka, va).reshape(n, aw)

        m, bc, cc, ar, ai = _s5_operators(ssm_a_re[layer], ssm_a_im[layer], ssm_log_dt[layer],
                                          ssm_b_re[layer], ssm_b_im[layer], ssm_c_re[layer],
                                          ssm_c_im[layer], ssm_d[layer])
        y = _s5(u.reshape(n // SSM_CHUNK, SSM_CHUNK * sw), m, bc, cc, ar, ai, b).reshape(n, sw)

        wconv = jnp.pad(conv_w[layer].astype(F32), ((0, CONV_HALO - CONV_K), (0, 0)))
        conv = _conv(hc, wconv, row2(conv_b[layer]), row2(conv_ln_g[layer]), row2(conv_ln_b[layer]), b, l)

        x2, h = _merge(x2, att, y, conv, gates, ssm_w_glu[layer].astype(BF16),
                       w_att_out[layer].astype(BF16), w_ssm_out[layer].astype(BF16),
                       w_conv_out[layer].astype(BF16), w_o[layer].astype(BF16), row2(g_ffn[layer]))

        i = layer // 2
        if layer % 2 == 0:
            x2 = _ffn(x2, h, ffn_w_gate[i].astype(BF16), ffn_w_up[i].astype(BF16), ffn_w_down[i].astype(BF16))
        else:
            wts = _router(x2, row2(g_ffn[layer]), _pad_cols(moe_w_router[i].astype(F32), LANES),
                          _pad_cols(row2(moe_b_router[i]), LANES))
            x2 = _moe(x2, h, wts, moe_w_gate[i].astype(BF16), moe_w_up[i].astype(BF16),
                      moe_w_down[i].astype(BF16))
    return _final_norm(x2, row2(g_final)).reshape(b, l, d)
```

```python
import functools
import math

import jax
import jax.numpy as jnp
from jax import lax
from jax.experimental import pallas as pl
from jax.experimental.pallas import tpu as pltpu

F32 = jnp.float32
BF16 = jnp.bfloat16
HIGHEST = lax.Precision.HIGHEST

EPS = 1e-6
ATT_HEADS = 8
ATT_HEAD_DIM = 64
SSM_GROUPS = 16
SSM_GROUP = 16
SSM_STATE = 64
CONV_K = 31
N_EXPERTS = 8
LANES = 128
HEAD_PAD = 128
SSM_CHUNK = 8
CONV_HALO = 32
NEG_BIG = -1e30

TOKEN_TILE = 512
ATT_TILE = 512
CONV_TILE = 512
CONV_ROWS = 64
MOE_TILE = 512
VMEM_LIMIT = 56 * 1024 * 1024


def _resident(shape):
    nd = len(shape)
    return pl.BlockSpec(shape, lambda *_: (0,) * nd, pipeline_mode=pl.Buffered(1))


def _sigmoid(x):
    return 1.0 / (1.0 + jnp.exp(-x))


def _silu(x):
    return x * _sigmoid(x)


def _gelu_tanh(x):
    c = math.sqrt(2.0 / math.pi)
    return 0.5 * x * (1.0 + jnp.tanh(c * (x + 0.044715 * (x * x * x))))


def _rms(x, g):
    return x * lax.rsqrt(jnp.mean(x * x, axis=-1, keepdims=True) + EPS) * g


def _bdot(a, b):
    return jnp.dot(a, b, preferred_element_type=F32)


def _inproj_kernel(x_ref, g_ref, wqkv_ref, bqkv_ref, wf_ref, bf_ref, wu_ref, bu_ref,
                   wc_ref, bc_ref, wg_ref, bg_ref,
                   qkv_ref, logf_ref, u_ref, hc_ref, gates_ref):
    h = _rms(x_ref[...], g_ref[...]).astype(BF16)
    qkv_ref[...] = (_bdot(h, wqkv_ref[...]) + bqkv_ref[...]).astype(BF16)
    fz = _bdot(h, wf_ref[...]) + bf_ref[...]
    logf_ref[...] = jnp.minimum(fz, 0.0) - jnp.log1p(jnp.exp(-jnp.abs(fz)))
    u_ref[...] = (_bdot(h, wu_ref[...]) + bu_ref[...]).astype(BF16)
    cz = _bdot(h, wc_ref[...]) + bc_ref[...]
    cw = cz.shape[-1] // 2
    hc_ref[...] = (cz[:, :cw] * _sigmoid(cz[:, cw:])).astype(BF16)
    d = x_ref.shape[-1]
    for j in range(gates_ref.shape[-1] // d):
        gz = _bdot(h, wg_ref[:, j * d:(j + 1) * d]) + bg_ref[:, j * d:(j + 1) * d]
        gates_ref[:, j * d:(j + 1) * d] = _sigmoid(gz).astype(BF16)


def _inproj(x2, g, wqkv, bqkv, wf, bf, wu, bu, wc, bc, wg, bg):
    n, d = x2.shape
    tm = min(TOKEN_TILE, n)
    row = lambda w: pl.BlockSpec((tm, w), lambda i: (i, 0))
    outs = [(wqkv.shape[1], BF16), (LANES, F32), (wu.shape[1], BF16), (wc.shape[1] // 2, BF16),
            (wg.shape[1], BF16)]
    return pl.pallas_call(
        _inproj_kernel,
        grid=(n // tm,),
        in_specs=[row(d)] + [_resident(a.shape) for a in (g, wqkv, bqkv, wf, bf, wu, bu, wc, bc, wg, bg)],
        out_specs=[row(w) for w, _ in outs],
        out_shape=[jax.ShapeDtypeStruct((n, w), dt) for w, dt in outs],
        compiler_params=pltpu.CompilerParams(dimension_semantics=("parallel",),
                                             vmem_limit_bytes=VMEM_LIMIT),
        name="inproj",
    )(x2, g, wqkv, bqkv, wf, bf, wu, bu, wc, bc, wg, bg)


def _split3(f):
    hi = f.astype(BF16).astype(F32)
    r = f - hi
    mid = r.astype(BF16).astype(F32)
    lo = (r - mid).astype(BF16).astype(F32)
    return hi, mid, lo


def _foxprep_kernel(qkv_ref, logf_ref, qa_ref, ka_ref, vt_ref, carry_ref):
    t = qkv_ref.shape[0]

    @pl.when(pl.program_id(1) == 0)
    def _():
        carry_ref[...] = jnp.zeros_like(carry_ref)

    r = lax.broadcasted_iota(jnp.int32, (t, t), 0)
    c = lax.broadcasted_iota(jnp.int32, (t, t), 1)
    tri = (c <= r).astype(F32)
    fcum = jnp.dot(tri, logf_ref[...], preferred_element_type=F32, precision=HIGHEST) + carry_ref[...]
    carry_ref[...] = fcum[t - 1:t, :]

    dh = ATT_HEAD_DIM
    aw = ATT_HEADS * dh
    lane = lax.broadcasted_iota(jnp.int32, (t, HEAD_PAD - dh), 1)
    for h in range(ATT_HEADS):
        hi, mid, lo = _split3(fcum[:, h:h + 1])
        one = jnp.where(lane < 6, 1.0, 0.0)
        qx = jnp.where(lane == 0, hi, jnp.where(lane == 1, mid, jnp.where(lane == 2, lo, one)))
        kx = jnp.where(lane == 3, -hi, jnp.where(lane == 4, -mid, jnp.where(lane == 5, -lo, one)))
        vx = jnp.where(lane == 0, 1.0, 0.0)
        q = qkv_ref[:, h * dh:(h + 1) * dh].astype(F32)
        k = qkv_ref[:, aw + h * dh:aw + (h + 1) * dh].astype(F32)
        v = qkv_ref[:, 2 * aw + h * dh:2 * aw + (h + 1) * dh].astype(F32)
        qa_ref[0, h] = jnp.concatenate([q, qx], axis=-1).astype(BF16)
        ka_ref[0, h] = jnp.concatenate([k, kx], axis=-1).astype(BF16)
        vt_ref[0, h] = jnp.concatenate([v, vx], axis=-1).T.astype(BF16)


def _foxprep(qkv, logf, b, l):
    t = min(ATT_TILE, l)
    nt = l // t
    out = jax.ShapeDtypeStruct((b, ATT_HEADS, l, HEAD_PAD), BF16)
    out_t = jax.ShapeDtypeStruct((b, ATT_HEADS, HEAD_PAD, l), BF16)
    ospec = pl.BlockSpec((1, ATT_HEADS, t, HEAD_PAD), lambda bi, i: (bi, 0, i, 0))
    ospec_t = pl.BlockSpec((1, ATT_HEADS, HEAD_PAD, t), lambda bi, i: (bi, 0, 0, i))
    return pl.pallas_call(
        _foxprep_kernel,
        grid=(b, nt),
        in_specs=[pl.BlockSpec((t, qkv.shape[1]), lambda bi, i: (bi * nt + i, 0)),
                  pl.BlockSpec((t, LANES), lambda bi, i: (bi * nt + i, 0))],
        out_specs=[ospec, ospec, ospec_t],
        out_shape=[out, out, out_t],
        scratch_shapes=[pltpu.VMEM((1, LANES), F32)],
        compiler_params=pltpu.CompilerParams(dimension_semantics=("parallel", "arbitrary"),
                                             vmem_limit_bytes=VMEM_LIMIT),
        name="foxprep",
    )(qkv, logf)


def _fox_kernel(qi_ref, ki_ref, qa_ref, ka_ref, vt_ref, o_ref, m_ref, acc_ref):
    p_id = pl.program_id(1)
    qi = qi_ref[p_id]
    ki = ki_ref[p_id]
    t = qa_ref.shape[2]
    dh = ATT_HEAD_DIM

    @pl.when(ki == 0)
    def _():
        m_ref[...] = jnp.full_like(m_ref, NEG_BIG)
        acc_ref[...] = jnp.zeros_like(acc_ref)

    def step(masked):
        for h in range(ATT_HEADS):
            st = lax.dot_general(ka_ref[0, h], qa_ref[0, h], (((1,), (1,)), ((), ())),
                                 preferred_element_type=F32)
            if masked:
                kpos = lax.broadcasted_iota(jnp.int32, (t, t), 0)
                qpos = lax.broadcasted_iota(jnp.int32, (t, t), 1)
                st = jnp.where(kpos <= qpos, st, NEG_BIG)
            m_prev = m_ref[h]
            m_new = jnp.maximum(m_prev, jnp.max(st, axis=0, keepdims=True))
            p = jnp.exp(st - m_new).astype(BF16)
            acc_ref[h] = jnp.exp(m_prev - m_new) * acc_ref[h] + _bdot(vt_ref[0, h], p)
            m_ref[h] = m_new

    @pl.when(ki < qi)
    def _():
        step(False)

    @pl.when(ki == qi)
    def _():
        step(True)
        for h in range(ATT_HEADS):
            a = acc_ref[h]
            o = (a / a[dh:dh + 1, :]).T
            o_ref[0, :, h * dh:(h + 1) * dh] = o[:, :dh].astype(BF16)


def _fox_attention(qa, ka, vt):
    b, nh, l, _ = qa.shape
    t = min(ATT_TILE, l)
    nt = l // t
    pairs = [(q, k) for q in range(nt) for k in range(q + 1)]
    qi = jnp.asarray([p[0] for p in pairs], jnp.int32)
    ki = jnp.asarray([p[1] for p in pairs], jnp.int32)
    qspec = pl.BlockSpec((1, nh, t, HEAD_PAD), lambda bi, p, qi, ki: (bi, 0, qi[p], 0))
    kspec = pl.BlockSpec((1, nh, t, HEAD_PAD), lambda bi, p, qi, ki: (bi, 0, ki[p], 0))
    vspec = pl.BlockSpec((1, nh, HEAD_PAD, t), lambda bi, p, qi, ki: (bi, 0, 0, ki[p]))
    return pl.pallas_call(
        _fox_kernel,
        grid_spec=pltpu.PrefetchScalarGridSpec(
            num_scalar_prefetch=2,
            grid=(b, len(pairs)),
            in_specs=[qspec, kspec, vspec],
            out_specs=pl.BlockSpec((1, t, nh * ATT_HEAD_DIM), lambda bi, p, qi, ki: (bi, qi[p], 0)),
            scratch_shapes=[pltpu.VMEM((nh, 1, t), F32), pltpu.VMEM((nh, HEAD_PAD, t), F32)],
        ),
        out_shape=jax.ShapeDtypeStruct((b, l, nh * ATT_HEAD_DIM), BF16),
        compiler_params=pltpu.CompilerParams(dimension_semantics=("parallel", "arbitrary"),
                                             vmem_limit_bytes=VMEM_LIMIT),
        name="fox_attention",
    )(qi, ki, qa, ka, vt)


def _s5_kernel(u_ref, m_ref, bc_ref, cc_ref, ar_ref, ai_ref, y_ref, v_ref, xp_ref):
    u = u_ref[...]
    rows = u.shape[0]
    half = ar_ref.shape[-1]
    v_ref[...] = _bdot(u, bc_ref[...])
    ar = ar_ref[...]
    ai = ai_ref[...]

    def group(i, carry):
        xr, xi = carry
        base = pl.multiple_of(i * 8, 8)
        blk = v_ref[pl.ds(base, 8), :]
        prev = []
        for j in range(8):
            prev.append(jnp.concatenate([xr, xi], axis=-1))
            vr = blk[j:j + 1, :half]
            vi = blk[j:j + 1, half:]
            xr, xi = ar * xr - ai * xi + vr, ar * xi + ai * xr + vi
        xp_ref[pl.ds(base, 8), :] = jnp.concatenate(prev, axis=0)
        return xr, xi

    zero = jnp.zeros((1, half), F32)
    lax.fori_loop(0, rows // 8, group, (zero, zero))
    y = _bdot(u, m_ref[...]) + _bdot(xp_ref[...].astype(BF16), cc_ref[...])
    y_ref[...] = y.astype(BF16)


def _s5(u2, m, bc, cc, ar, ai, b):
    rows_total, w = u2.shape
    rows = rows_total // b
    return pl.pallas_call(
        _s5_kernel,
        grid=(b,),
        in_specs=[pl.BlockSpec((rows, w), lambda i: (i, 0))] + [_resident(a.shape) for a in (m, bc, cc, ar, ai)],
        out_specs=pl.BlockSpec((rows, w), lambda i: (i, 0)),
        out_shape=jax.ShapeDtypeStruct((rows_total, w), BF16),
        scratch_shapes=[pltpu.VMEM((rows, bc.shape[1]), F32), pltpu.VMEM((rows, bc.shape[1]), F32)],
        compiler_params=pltpu.CompilerParams(dimension_semantics=("parallel",),
                                             vmem_limit_bytes=VMEM_LIMIT),
        name="s5",
    )(u2, m, bc, cc, ar, ai)


def _s5_operators(a_re, a_im, log_dt, b_re, b_im, c_re, c_im, d_skip):
    t = SSM_CHUNK
    g, p = a_re.shape
    hc = d_skip.shape[-1]
    lam = lax.complex(a_re.astype(F32), a_im.astype(F32))
    dt = jnp.exp(log_dt.astype(F32))[:, None]
    a_bar = jnp.exp(lam * dt)
    b_bar = ((a_bar - 1.0) / lam)[..., None] * lax.complex(b_re.astype(F32), b_im.astype(F32))
    c = lax.complex(c_re.astype(F32), c_im.astype(F32))
    tau = jnp.arange(t + 1, dtype=F32)
    apow = jnp.exp((lam * dt)[None] * tau[:, None, None])
    eye_g = jnp.eye(g, dtype=F32)
    kern = jnp.real(jnp.einsum('gop,tgp,gpi->tgio', c, apow[:t], b_bar))
    kern = kern.at[0].add(jnp.eye(hc, dtype=F32)[None] * d_skip.astype(F32)[:, :, None])
    s_idx = jnp.arange(t)[:, None]
    t_idx = jnp.arange(t)[None, :]
    diff = t_idx - s_idx
    kst = jnp.where((diff >= 0)[:, :, None, None, None], kern[jnp.clip(diff, 0, t - 1)], 0.0)
    m = jnp.einsum('stgio,gk->sgitko', kst, eye_g).reshape(t * g * hc, t * g * hc)
    bs = apow[:t][::-1][:, :, :, None] * b_bar[None]
    bcx = jnp.einsum('sgph,gk->sghkp', bs, eye_g.astype(bs.dtype)).reshape(t * g * hc, g * p)
    bc = jnp.concatenate([jnp.real(bcx), jnp.imag(bcx)], axis=1)
    ct = c[None] * apow[1:][:, :, None, :]
    ccx = jnp.einsum('tgop,gk->gptko', ct, eye_g.astype(ct.dtype)).reshape(g * p, t * g * hc)
    cc = jnp.concatenate([jnp.real(ccx), -jnp.imag(ccx)], axis=0)
    a_t = apow[t].reshape(1, g * p)
    return (m.astype(BF16), bc.astype(BF16), cc.astype(BF16),
            jnp.real(a_t).astype(F32), jnp.imag(a_t).astype(F32))


def _conv_kernel(h_ref, w_ref, b_ref, g_ref, beta_ref, o_ref, buf_ref):
    t = h_ref.shape[0]

    @pl.when(pl.program_id(1) == 0)
    def _():
        buf_ref[0:CONV_HALO, :] = jnp.zeros((CONV_HALO, buf_ref.shape[1]), F32)

    @pl.when(pl.program_id(1) > 0)
    def _():
        buf_ref[0:CONV_HALO, :] = buf_ref[t:t + CONV_HALO, :]

    buf_ref[CONV_HALO:, :] = h_ref[...].astype(F32)
    off = CONV_HALO - (CONV_K - 1)
    rb = min(CONV_ROWS, t)
    for r0 in range(0, t, rb):
        acc = jnp.zeros((rb, h_ref.shape[1]), F32)
        for j in range(CONV_K):
            acc = acc + w_ref[j:j + 1, :] * buf_ref[r0 + off + j:r0 + off + j + rb, :]
        y = acc + b_ref[...]
        mu = jnp.mean(y, axis=-1, keepdims=True)
        yc = y - mu
        var = jnp.mean(yc * yc, axis=-1, keepdims=True)
        z = yc * lax.rsqrt(var + EPS) * g_ref[...] + beta_ref[...]
        o_ref[r0:r0 + rb, :] = _silu(z).astype(BF16)


def _conv(hc, w, bias, g, beta, b, l):
    t = min(CONV_TILE, l)
    nt = l // t
    cw = hc.shape[1]
    return pl.pallas_call(
        _conv_kernel,
        grid=(b, nt),
        in_specs=[pl.BlockSpec((t, cw), lambda bi, i: (bi * nt + i, 0))]
                 + [_resident(a.shape) for a in (w, bias, g, beta)],
        out_specs=pl.BlockSpec((t, cw), lambda bi, i: (bi * nt + i, 0)),
        out_shape=jax.ShapeDtypeStruct(hc.shape, BF16),
        scratch_shapes=[pltpu.VMEM((t + CONV_HALO, cw), F32)],
        compiler_params=pltpu.CompilerParams(dimension_semantics=("parallel", "arbitrary"),
                                             vmem_limit_bytes=VMEM_LIMIT),
        name="conv",
    )(hc, w, bias, g, beta)


def _pack_halves(h):
    w = h.shape[-1] // 2
    bits = lax.bitcast_convert_type(h.astype(BF16).astype(F32), jnp.uint32)
    return bits[:, :w] | (bits[:, w:] >> 16)


def _unpack_halves(u):
    hi = lax.bitcast_convert_type(u & jnp.uint32(0xFFFF0000), F32)
    lo = lax.bitcast_convert_type(u << 16, F32)
    return jnp.concatenate([hi, lo], axis=-1).astype(BF16)


def _merge_kernel(x_ref, att_ref, y_ref, conv_ref, gates_ref, wglu_ref, wa_ref, ws_ref, wc_ref, wo_ref,
                  gf_ref, xo_ref, h_ref, *, packed):
    d = x_ref.shape[-1]
    sw = y_ref.shape[-1]
    yg = _bdot(_gelu_tanh(y_ref[...].astype(F32)).astype(BF16), wglu_ref[...])
    ssm = (yg[:, :sw] * _sigmoid(yg[:, sw:])).astype(BF16)
    merged = gates_ref[:, 0:d].astype(F32) * _bdot(att_ref[...], wa_ref[...])
    merged = merged + gates_ref[:, d:2 * d].astype(F32) * _bdot(ssm, ws_ref[...])
    merged = merged + gates_ref[:, 2 * d:3 * d].astype(F32) * _bdot(conv_ref[...], wc_ref[...])
    xn = x_ref[...] + _bdot(merged.astype(BF16), wo_ref[...])
    xo_ref[...] = xn
    hn = _rms(xn, gf_ref[...])
    h_ref[...] = _pack_halves(hn) if packed else hn.astype(BF16)


def _merge(x2, att, y, conv, gates, wglu, wa, ws, wc, wo, gf, packed):
    n, d = x2.shape
    tm = min(TOKEN_TILE, n)
    row = lambda w: pl.BlockSpec((tm, w), lambda i: (i, 0))
    hw, hdt = (d // 2, jnp.uint32) if packed else (d, BF16)
    return pl.pallas_call(
        functools.partial(_merge_kernel, packed=packed),
        grid=(n // tm,),
        in_specs=[row(d), row(att.shape[1]), row(y.shape[1]), row(conv.shape[1]), row(gates.shape[1])]
                 + [_resident(a.shape) for a in (wglu, wa, ws, wc, wo, gf)],
        out_specs=[row(d), row(hw)],
        out_shape=[jax.ShapeDtypeStruct((n, d), F32), jax.ShapeDtypeStruct((n, hw), hdt)],
        compiler_params=pltpu.CompilerParams(dimension_semantics=("parallel",),
                                             vmem_limit_bytes=VMEM_LIMIT),
        name="merge",
    )(x2, att, y, conv, gates, wglu, wa, ws, wc, wo, gf)


def _ffn_kernel(x_ref, h_ref, wg_ref, wu_ref, wd_ref, xo_ref, *, chunks):
    h = h_ref[...]
    f = wg_ref.shape[1]
    fc = f // chunks
    acc = x_ref[...]
    for j in range(chunks):
        sl = slice(j * fc, (j + 1) * fc)
        act = (_silu(_bdot(h, wg_ref[:, sl])) * _bdot(h, wu_ref[:, sl])).astype(BF16)
        acc = acc + _bdot(act, wd_ref[sl, :])
    xo_ref[...] = acc


def _ffn(x2, h, wg, wu, wd):
    n, d = x2.shape
    tm = min(TOKEN_TILE, n)
    row = pl.BlockSpec((tm, d), lambda i: (i, 0))
    f = wg.shape[1]
    chunks = 2 if f % (2 * LANES) == 0 else 1
    return pl.pallas_call(
        functools.partial(_ffn_kernel, chunks=chunks),
        grid=(n // tm,),
        in_specs=[row, row] + [_resident(a.shape) for a in (wg, wu, wd)],
        out_specs=row,
        out_shape=jax.ShapeDtypeStruct((n, d), F32),
        compiler_params=pltpu.CompilerParams(dimension_semantics=("parallel",),
                                             vmem_limit_bytes=VMEM_LIMIT),
        name="ffn",
    )(x2, h, wg, wu, wd)


META_E1, META_E2, META_P1, META_P2, META_R1, META_R2 = range(6)


def _router_kernel(x_ref, g_ref, wr_ref, br_ref, meta_ref, counts_ref):
    @pl.when(pl.program_id(0) == 0)
    def _():
        counts_ref[...] = jnp.zeros_like(counts_ref)

    h = _rms(x_ref[...], g_ref[...])
    logits = jnp.dot(h, wr_ref[...], preferred_element_type=F32, precision=HIGHEST) + br_ref[...]
    tm = logits.shape[0]
    lane = lax.broadcasted_iota(jnp.int32, logits.shape, 1)
    logits = jnp.where(lane < N_EXPERTS, logits, NEG_BIG)
    m1 = jnp.max(logits, axis=-1, keepdims=True)
    i1 = jnp.min(jnp.where(logits == m1, lane, LANES), axis=-1, keepdims=True)
    rest = jnp.where(lane == i1, NEG_BIG, logits)
    m2 = jnp.max(rest, axis=-1, keepdims=True)
    i2 = jnp.min(jnp.where(rest == m2, lane, LANES), axis=-1, keepdims=True)
    e = jnp.exp(m2 - m1)
    p1 = 1.0 / (1.0 + e)
    p2 = e / (1.0 + e)
    assigned = jnp.where((lane == i1) | (lane == i2), 1.0, 0.0)
    r = lax.broadcasted_iota(jnp.int32, (tm, tm), 0)
    c = lax.broadcasted_iota(jnp.int32, (tm, tm), 1)
    before = _bdot((c < r).astype(BF16), assigned.astype(BF16)) + counts_ref[...]
    rank1 = jnp.sum(jnp.where(lane == i1, before, 0.0), axis=-1, keepdims=True)
    rank2 = jnp.sum(jnp.where(lane == i2, before, 0.0), axis=-1, keepdims=True)
    meta = jnp.where(lane == META_E1, i1.astype(F32), 0.0)
    for idx, val in ((META_E2, i2.astype(F32)), (META_P1, p1), (META_P2, p2), (META_R1, rank1), (META_R2, rank2)):
        meta = jnp.where(lane == idx, val, meta)
    meta_ref[...] = meta
    counts_ref[...] += jnp.sum(assigned, axis=0, keepdims=True)


def _router(x2, g, wr, br):
    n, d = x2.shape
    tm = min(TOKEN_TILE, n)
    return pl.pallas_call(
        _router_kernel,
        grid=(n // tm,),
        in_specs=[pl.BlockSpec((tm, d), lambda i: (i, 0))] + [_resident(a.shape) for a in (g, wr, br)],
        out_specs=[pl.BlockSpec((tm, LANES), lambda i: (i, 0)), pl.BlockSpec((1, LANES), lambda i: (0, 0))],
        out_shape=[jax.ShapeDtypeStruct((n, LANES), F32), jax.ShapeDtypeStruct((1, LANES), F32)],
        compiler_params=pltpu.CompilerParams(dimension_semantics=("arbitrary",),
                                             vmem_limit_bytes=VMEM_LIMIT),
        name="router",
    )(x2, g, wr, br)


def _dispatch_plan(meta, counts, tm, n_tiles):
    cnt = counts[0, :N_EXPERTS].astype(jnp.int32)
    tiles = (cnt + tm - 1) // tm
    tile_end = jnp.cumsum(tiles)
    row_start = (tile_end - tiles) * tm
    t_idx = jnp.arange(n_tiles, dtype=jnp.int32)
    n_used = tile_end[-1]
    tile_expert = jnp.sum((t_idx[:, None] >= tile_end[None, :]).astype(jnp.int32), axis=1)
    last_expert = jnp.sum((n_used - 1 >= tile_end).astype(jnp.int32))
    tile_expert = jnp.where(t_idx < n_used, tile_expert, last_expert)
    experts = jnp.arange(N_EXPERTS, dtype=jnp.int32)

    def pos(e_lane, r_lane):
        e = meta[:, e_lane].astype(jnp.int32)
        start = jnp.sum(jnp.where(e[:, None] == experts[None, :], row_start[None, :], 0), axis=1)
        return start + meta[:, r_lane].astype(jnp.int32)

    pos2 = jnp.stack([pos(META_E1, META_R1), pos(META_E2, META_R2)], axis=1)
    return pos2, tile_expert, n_used.reshape(1)


def _dispatch_kernel(pos_ref, h_ref, init_ref, sorted_ref, sem):
    del init_ref
    tm = h_ref.shape[0]

    def row_copy(r, k):
        return pltpu.make_async_copy(h_ref.at[pl.ds(r, 1), :],
                                     sorted_ref.at[pl.ds(pos_ref[0, 0, 2 * r + k], 1), :], sem)

    def start(r, carry):
        row_copy(r, 0).start()
        row_copy(r, 1).start()
        return carry

    def wait(r, carry):
        row_copy(r, 0).wait()
        row_copy(r, 1).wait()
        return carry

    lax.fori_loop(0, tm, start, 0)
    lax.fori_loop(0, tm, wait, 0)


def _dispatch(hp, pos2, n_rows):
    n, w = hp.shape
    tm = min(MOE_TILE, n)
    pos_tiles = pos2.reshape(n // tm, 1, 2 * tm)
    return pl.pallas_call(
        _dispatch_kernel,
        grid=(n // tm,),
        in_specs=[pl.BlockSpec((1, 1, 2 * tm), lambda i: (i, 0, 0), memory_space=pltpu.SMEM),
                  pl.BlockSpec((tm, w), lambda i: (i, 0)),
                  pl.BlockSpec(memory_space=pl.ANY)],
        out_specs=pl.BlockSpec(memory_space=pl.ANY),
        out_shape=jax.ShapeDtypeStruct((n_rows, w), hp.dtype),
        scratch_shapes=[pltpu.SemaphoreType.DMA(())],
        input_output_aliases={2: 0},
        compiler_params=pltpu.CompilerParams(dimension_semantics=("arbitrary",),
                                             vmem_limit_bytes=VMEM_LIMIT),
        name="dispatch",
    )(pos_tiles, hp, jnp.zeros((n_rows, w), hp.dtype))


def _gffn_kernel(te_ref, nu_ref, rows_ref, wg_ref, wu_ref, wd_ref, y_ref, *, chunks):
    del te_ref

    @pl.when(pl.program_id(0) < nu_ref[0])
    def _():
        h = _unpack_halves(rows_ref[...])
        fc = wg_ref.shape[2] // chunks
        acc = None
        for j in range(chunks):
            sl = slice(j * fc, (j + 1) * fc)
            act = (_silu(_bdot(h, wg_ref[0, :, sl])) * _bdot(h, wu_ref[0, :, sl])).astype(BF16)
            part = _bdot(act, wd_ref[0, sl, :])
            acc = part if acc is None else acc + part
        y_ref[...] = acc

    @pl.when(pl.program_id(0) >= nu_ref[0])
    def _():
        y_ref[...] = jnp.zeros_like(y_ref)


def _gffn(rows, tile_expert, n_used, wg, wu, wd):
    n_rows, w = rows.shape
    ne, d, f = wg.shape
    tm = min(MOE_TILE, n_rows)
    chunks = 2 if f % (2 * LANES) == 0 else 1
    wspec = lambda shape: pl.BlockSpec(shape, lambda t, te, nu: (te[t], 0, 0), pipeline_mode=pl.Buffered(1))
    return pl.pallas_call(
        functools.partial(_gffn_kernel, chunks=chunks),
        grid_spec=pltpu.PrefetchScalarGridSpec(
            num_scalar_prefetch=2,
            grid=(n_rows // tm,),
            in_specs=[pl.BlockSpec((tm, w), lambda t, te, nu: (t, 0)),
                      wspec((1, d, f)), wspec((1, d, f)), wspec((1, f, d))],
            out_specs=pl.BlockSpec((tm, d), lambda t, te, nu: (t, 0)),
        ),
        out_shape=jax.ShapeDtypeStruct((n_rows, d), F32),
        compiler_params=pltpu.CompilerParams(dimension_semantics=("arbitrary",),
                                             vmem_limit_bytes=VMEM_LIMIT),
        name="grouped_ffn",
    )(tile_expert, n_used, rows, wg, wu, wd)


def _combine_kernel(pos_ref, x_ref, meta_ref, g_ref, y_ref, o_ref, buf1_ref, buf2_ref, sem, *, final):
    tm = x_ref.shape[0]
    bufs = (buf1_ref, buf2_ref)

    def row_copy(r, k):
        return pltpu.make_async_copy(y_ref.at[pl.ds(pos_ref[0, 0, 2 * r + k], 1), :],
                                     bufs[k].at[pl.ds(r, 1), :], sem)

    def start(r, carry):
        row_copy(r, 0).start()
        row_copy(r, 1).start()
        return carry

    def wait(r, carry):
        row_copy(r, 0).wait()
        row_copy(r, 1).wait()
        return carry

    lax.fori_loop(0, tm, start, 0)
    lax.fori_loop(0, tm, wait, 0)
    meta = meta_ref[...]
    p1 = meta[:, META_P1:META_P1 + 1]
    p2 = meta[:, META_P2:META_P2 + 1]
    xn = x_ref[...] + p1 * buf1_ref[...] + p2 * buf2_ref[...]
    o_ref[...] = _rms(xn, g_ref[...]) if final else xn


def _combine(x2, meta, pos2, y, g, final):
    n, d = x2.shape
    tm = min(MOE_TILE, n)
    pos_tiles = pos2.reshape(n // tm, 1, 2 * tm)
    return pl.pallas_call(
        functools.partial(_combine_kernel, final=final),
        grid=(n // tm,),
        in_specs=[pl.BlockSpec((1, 1, 2 * tm), lambda i: (i, 0, 0), memory_space=pltpu.SMEM),
                  pl.BlockSpec((tm, d), lambda i: (i, 0)),
                  pl.BlockSpec((tm, LANES), lambda i: (i, 0)),
                  _resident(g.shape),
                  pl.BlockSpec(memory_space=pl.ANY)],
        out_specs=pl.BlockSpec((tm, d), lambda i: (i, 0)),
        out_shape=jax.ShapeDtypeStruct((n, d), F32),
        scratch_shapes=[pltpu.VMEM((tm, d), F32), pltpu.VMEM((tm, d), F32), pltpu.SemaphoreType.DMA(())],
        compiler_params=pltpu.CompilerParams(dimension_semantics=("arbitrary",),
                                             vmem_limit_bytes=VMEM_LIMIT),
        name="combine",
    )(pos_tiles, x2, meta, g, y)


def _moe(x2, hp, g_ffn_row, wr, br, wg, wu, wd, g_out, final):
    n, _ = x2.shape
    tm = min(MOE_TILE, n)
    n_tiles = 2 * n // tm + N_EXPERTS
    meta, counts = _router(x2, g_ffn_row, wr, br)
    pos2, tile_expert, n_used = _dispatch_plan(meta, counts, tm, n_tiles)
    rows = _dispatch(hp, pos2, n_tiles * tm)
    y = _gffn(rows, tile_expert, n_used, wg, wu, wd)
    return _combine(x2, meta, pos2, y, g_out, final)


def _final_kernel(x_ref, g_ref, o_ref):
    o_ref[...] = _rms(x_ref[...], g_ref[...])


def _final_norm(x2, g):
    n, d = x2.shape
    tm = min(TOKEN_TILE, n)
    row = pl.BlockSpec((tm, d), lambda i: (i, 0))
    return pl.pallas_call(
        _final_kernel,
        grid=(n // tm,),
        in_specs=[row, _resident(g.shape)],
        out_specs=row,
        out_shape=jax.ShapeDtypeStruct((n, d), F32),
        compiler_params=pltpu.CompilerParams(dimension_semantics=("parallel",),
                                             vmem_limit_bytes=VMEM_LIMIT),
        name="final_norm",
    )(x2, g)


def _pad_cols(a, width):
    return jnp.pad(a, ((0, 0), (0, width - a.shape[1])))


def kernel(x, g_mix, w_in, b_in, ssm_a_re, ssm_a_im, ssm_log_dt, ssm_b_re, ssm_b_im, ssm_c_re, ssm_c_im, ssm_d, ssm_w_glu, conv_w, conv_b, conv_ln_g, conv_ln_b, w_att_out, w_ssm_out, w_conv_out, w_o, g_ffn, ffn_w_gate, ffn_w_up, ffn_w_down, moe_w_router, moe_b_router, moe_w_gate, moe_w_up, moe_w_down, g_final):
    b, l, d = x.shape
    depth = w_in.shape[0]
    n = b * l
    aw = ATT_HEADS * ATT_HEAD_DIM
    sw = SSM_GROUPS * SSM_GROUP
    cw = conv_w.shape[-1]
    q_end, k_end, v_end = aw, 2 * aw, 3 * aw
    f_end = v_end + ATT_HEADS
    u_end = f_end + sw
    c_end = u_end + 2 * cw
    scale = ATT_HEAD_DIM ** -0.5
    col_scale = jnp.concatenate([jnp.full((aw,), scale, F32), jnp.ones((2 * aw,), F32)])
    row2 = lambda v: v.reshape(1, -1).astype(F32)

    x2 = x.reshape(n, d)
    for layer in range(depth):
        w = w_in[layer]
        bias = b_in[layer]
        wqkv = (w[:, :v_end] * col_scale).astype(BF16)
        bqkv = row2(bias[:v_end] * col_scale)
        wf = _pad_cols(w[:, v_end:f_end], LANES).astype(BF16)
        bf = _pad_cols(row2(bias[v_end:f_end]), LANES)
        qkv, logf, u, hc, gates = _inproj(
            x2, row2(g_mix[layer]), wqkv, bqkv, wf, bf,
            w[:, f_end:u_end].astype(BF16), row2(bias[f_end:u_end]),
            w[:, u_end:c_end].astype(BF16), row2(bias[u_end:c_end]),
            w[:, c_end:].astype(BF16), row2(bias[c_end:]))

        qa, ka, va = _foxprep(qkv, logf, b, l)
        att = _fox_attention(qa, ka, va).reshape(n, aw)

        m, bc, cc, ar, ai = _s5_operators(ssm_a_re[layer], ssm_a_im[layer], ssm_log_dt[layer],
                                          ssm_b_re[layer], ssm_b_im[layer], ssm_c_re[layer],
                                          ssm_c_im[layer], ssm_d[layer])
        y = _s5(u.reshape(n // SSM_CHUNK, SSM_CHUNK * sw), m, bc, cc, ar, ai, b).reshape(n, sw)

        wconv = jnp.pad(conv_w[layer].astype(F32), ((0, CONV_HALO - CONV_K), (0, 0)))
        conv = _conv(hc, wconv, row2(conv_b[layer]), row2(conv_ln_g[layer]), row2(conv_ln_b[layer]), b, l)

        is_moe = layer % 2 == 1
        last = layer == depth - 1
        x2, h = _merge(x2, att, y, conv, gates, ssm_w_glu[layer].astype(BF16),
                       w_att_out[layer].astype(BF16), w_ssm_out[layer].astype(BF16),
                       w_conv_out[layer].astype(BF16), w_o[layer].astype(BF16), row2(g_ffn[layer]), is_moe)

        i = layer // 2
        if is_moe:
            x2 = _moe(x2, h, row2(g_ffn[layer]), _pad_cols(moe_w_router[i].astype(F32), LANES),
                      _pad_cols(row2(moe_b_router[i]), LANES), moe_w_gate[i].astype(BF16),
                      moe_w_up[i].astype(BF16), moe_w_down[i].astype(BF16), row2(g_final), last)
        else:
            x2 = _ffn(x2, h, ffn_w_gate[i].astype(BF16), ffn_w_up[i].astype(BF16), ffn_w_down[i].astype(BF16))
            if last:
                x2 = _final_norm(x2, row2(g_final))
    return x2.reshape(b, l, d)
```

```python
import functools
import math

import jax
import jax.numpy as jnp
from jax import lax
from jax.experimental import pallas as pl
from jax.experimental.pallas import tpu as pltpu

F32 = jnp.float32
BF16 = jnp.bfloat16
HIGHEST = lax.Precision.HIGHEST

EPS = 1e-6
ATT_HEADS = 8
ATT_HEAD_DIM = 64
SSM_GROUPS = 16
SSM_GROUP = 16
SSM_STATE = 64
CONV_K = 31
N_EXPERTS = 8
LANES = 128
HEAD_PAD = 128
SSM_CHUNK = 8
CONV_HALO = 32
NEG_BIG = -1e30

TOKEN_TILE = 512
ATT_TILE = 512
CONV_TILE = 512
CONV_ROWS = 64
MOE_TILE = 512
DMA_UNROLL = 8
VMEM_LIMIT = 56 * 1024 * 1024


def _resident(shape):
    nd = len(shape)
    return pl.BlockSpec(shape, lambda *_: (0,) * nd, pipeline_mode=pl.Buffered(1))


def _sigmoid(x):
    return 1.0 / (1.0 + jnp.exp(-x))


def _silu(x):
    return x * _sigmoid(x)


def _gelu_tanh(x):
    c = math.sqrt(2.0 / math.pi)
    return 0.5 * x * (1.0 + jnp.tanh(c * (x + 0.044715 * (x * x * x))))


def _rms(x, g):
    return x * lax.rsqrt(jnp.mean(x * x, axis=-1, keepdims=True) + EPS) * g


def _bdot(a, b):
    return jnp.dot(a, b, preferred_element_type=F32)


def _inproj_kernel(x_ref, g_ref, wqkv_ref, bqkv_ref, wf_ref, bf_ref, wu_ref, bu_ref,
                   wc_ref, bc_ref, wg_ref, bg_ref,
                   qkv_ref, logf_ref, ua_ref, ub_ref, hc_ref, gates_ref):
    h = _rms(x_ref[...], g_ref[...]).astype(BF16)
    qkv_ref[...] = (_bdot(h, wqkv_ref[...]) + bqkv_ref[...]).astype(BF16)
    fz = _bdot(h, wf_ref[...]) + bf_ref[...]
    logf_ref[...] = jnp.minimum(fz, 0.0) - jnp.log1p(jnp.exp(-jnp.abs(fz)))
    u = _bdot(h, wu_ref[...]) + bu_ref[...]
    ua_ref[...] = u[:, :LANES]
    ub_ref[...] = u[:, LANES:]
    cz = _bdot(h, wc_ref[...]) + bc_ref[...]
    cw = cz.shape[-1] // 2
    hc_ref[...] = (cz[:, :cw] * _sigmoid(cz[:, cw:])).astype(BF16)
    d = x_ref.shape[-1]
    for j in range(gates_ref.shape[-1] // d):
        gz = _bdot(h, wg_ref[:, j * d:(j + 1) * d]) + bg_ref[:, j * d:(j + 1) * d]
        gates_ref[:, j * d:(j + 1) * d] = _sigmoid(gz).astype(BF16)


def _inproj(x2, g, wqkv, bqkv, wf, bf, wu, bu, wc, bc, wg, bg):
    n, d = x2.shape
    tm = min(TOKEN_TILE, n)
    row = lambda w: pl.BlockSpec((tm, w), lambda i: (i, 0))
    outs = [(wqkv.shape[1], BF16), (LANES, F32), (LANES, F32), (LANES, F32), (wc.shape[1] // 2, BF16),
            (wg.shape[1], BF16)]
    return pl.pallas_call(
        _inproj_kernel,
        grid=(n // tm,),
        in_specs=[row(d)] + [_resident(a.shape) for a in (g, wqkv, bqkv, wf, bf, wu, bu, wc, bc, wg, bg)],
        out_specs=[row(w) for w, _ in outs],
        out_shape=[jax.ShapeDtypeStruct((n, w), dt) for w, dt in outs],
        compiler_params=pltpu.CompilerParams(dimension_semantics=("parallel",),
                                             vmem_limit_bytes=VMEM_LIMIT),
        name="inproj",
    )(x2, g, wqkv, bqkv, wf, bf, wu, bu, wc, bc, wg, bg)


def _split3(f):
    hi = f.astype(BF16).astype(F32)
    r = f - hi
    mid = r.astype(BF16).astype(F32)
    lo = (r - mid).astype(BF16).astype(F32)
    return hi, mid, lo


def _foxprep_kernel(qkv_ref, logf_ref, qa_ref, ka_ref, vt_ref, carry_ref):
    t = qkv_ref.shape[0]

    @pl.when(pl.program_id(1) == 0)
    def _():
        carry_ref[...] = jnp.zeros_like(carry_ref)

    r = lax.broadcasted_iota(jnp.int32, (t, t), 0)
    c = lax.broadcasted_iota(jnp.int32, (t, t), 1)
    tri = (c <= r).astype(F32)
    fcum = jnp.dot(tri, logf_ref[...], preferred_element_type=F32, precision=HIGHEST) + carry_ref[...]
    carry_ref[...] = fcum[t - 1:t, :]

    dh = ATT_HEAD_DIM
    aw = ATT_HEADS * dh
    lane = lax.broadcasted_iota(jnp.int32, (t, HEAD_PAD - dh), 1)
    for h in range(ATT_HEADS):
        hi, mid, lo = _split3(fcum[:, h:h + 1])
        one = jnp.where(lane < 6, 1.0, 0.0)
        qx = jnp.where(lane == 0, hi, jnp.where(lane == 1, mid, jnp.where(lane == 2, lo, one)))
        kx = jnp.where(lane == 3, -hi, jnp.where(lane == 4, -mid, jnp.where(lane == 5, -lo, one)))
        vx = jnp.where(lane == 0, 1.0, 0.0)
        q = qkv_ref[:, h * dh:(h + 1) * dh].astype(F32)
        k = qkv_ref[:, aw + h * dh:aw + (h + 1) * dh].astype(F32)
        v = qkv_ref[:, 2 * aw + h * dh:2 * aw + (h + 1) * dh].astype(F32)
        qa_ref[0, h] = jnp.concatenate([q, qx], axis=-1).astype(BF16)
        ka_ref[0, h] = jnp.concatenate([k, kx], axis=-1).astype(BF16)
        vt_ref[0, h] = jnp.concatenate([v, vx], axis=-1).T.astype(BF16)


def _foxprep(qkv, logf, b, l):
    t = min(ATT_TILE, l)
    nt = l // t
    out = jax.ShapeDtypeStruct((b, ATT_HEADS, l, HEAD_PAD), BF16)
    out_t = jax.ShapeDtypeStruct((b, ATT_HEADS, HEAD_PAD, l), BF16)
    ospec = pl.BlockSpec((1, ATT_HEADS, t, HEAD_PAD), lambda bi, i: (bi, 0, i, 0))
    ospec_t = pl.BlockSpec((1, ATT_HEADS, HEAD_PAD, t), lambda bi, i: (bi, 0, 0, i))
    return pl.pallas_call(
        _foxprep_kernel,
        grid=(b, nt),
        in_specs=[pl.BlockSpec((t, qkv.shape[1]), lambda bi, i: (bi * nt + i, 0)),
                  pl.BlockSpec((t, LANES), lambda bi, i: (bi * nt + i, 0))],
        out_specs=[ospec, ospec, ospec_t],
        out_shape=[out, out, out_t],
        scratch_shapes=[pltpu.VMEM((1, LANES), F32)],
        compiler_params=pltpu.CompilerParams(dimension_semantics=("parallel", "arbitrary"),
                                             vmem_limit_bytes=VMEM_LIMIT),
        name="foxprep",
    )(qkv, logf)


def _fox_kernel(qi_ref, ki_ref, qa_ref, ka_ref, vt_ref, o_ref, m_ref, acc_ref, s_ref):
    p_id = pl.program_id(1)
    qi = qi_ref[p_id]
    ki = ki_ref[p_id]
    t = qa_ref.shape[2]
    dh = ATT_HEAD_DIM

    @pl.when(ki == 0)
    def _():
        m_ref[...] = jnp.full_like(m_ref, NEG_BIG)
        acc_ref[...] = jnp.zeros_like(acc_ref)

    def scores(h, masked):
        st = lax.dot_general(ka_ref[0, h], qa_ref[0, h], (((1,), (1,)), ((), ())),
                             preferred_element_type=F32)
        if masked:
            kpos = lax.broadcasted_iota(jnp.int32, (t, t), 0)
            qpos = lax.broadcasted_iota(jnp.int32, (t, t), 1)
            st = jnp.where(kpos <= qpos, st, NEG_BIG)
        s_ref[h % 2] = st

    def step(masked):
        scores(0, masked)
        for h in range(ATT_HEADS):
            if h + 1 < ATT_HEADS:
                scores(h + 1, masked)
            st = s_ref[h % 2]
            m_prev = m_ref[h]
            m_new = jnp.maximum(m_prev, jnp.max(st, axis=0, keepdims=True))
            p = jnp.exp(st - m_new).astype(BF16)
            acc_ref[h] = jnp.exp(m_prev - m_new) * acc_ref[h] + _bdot(vt_ref[0, h], p)
            m_ref[h] = m_new

    @pl.when(ki < qi)
    def _():
        step(False)

    @pl.when(ki == qi)
    def _():
        step(True)
        for h in range(ATT_HEADS):
            a = acc_ref[h]
            o = (a / a[dh:dh + 1, :]).T
            o_ref[0, :, h * dh:(h + 1) * dh] = o[:, :dh].astype(BF16)


def _fox_attention(qa, ka, vt):
    b, nh, l, _ = qa.shape
    t = min(ATT_TILE, l)
    nt = l // t
    pairs = [(q, k) for q in range(nt) for k in range(q + 1)]
    qi = jnp.asarray([p[0] for p in pairs], jnp.int32)
    ki = jnp.asarray([p[1] for p in pairs], jnp.int32)
    qspec = pl.BlockSpec((1, nh, t, HEAD_PAD), lambda bi, p, qi, ki: (bi, 0, qi[p], 0))
    kspec = pl.BlockSpec((1, nh, t, HEAD_PAD), lambda bi, p, qi, ki: (bi, 0, ki[p], 0))
    vspec = pl.BlockSpec((1, nh, HEAD_PAD, t), lambda bi, p, qi, ki: (bi, 0, 0, ki[p]))
    return pl.pallas_call(
        _fox_kernel,
        grid_spec=pltpu.PrefetchScalarGridSpec(
            num_scalar_prefetch=2,
            grid=(b, len(pairs)),
            in_specs=[qspec, kspec, vspec],
            out_specs=pl.BlockSpec((1, t, nh * ATT_HEAD_DIM), lambda bi, p, qi, ki: (bi, qi[p], 0)),
            scratch_shapes=[pltpu.VMEM((nh, 1, t), F32), pltpu.VMEM((nh, HEAD_PAD, t), F32),
                            pltpu.VMEM((2, t, t), F32)],
        ),
        out_shape=jax.ShapeDtypeStruct((b, l, nh * ATT_HEAD_DIM), BF16),
        compiler_params=pltpu.CompilerParams(dimension_semantics=("parallel", "arbitrary"),
                                             vmem_limit_bytes=VMEM_LIMIT),
        name="fox_attention",
    )(qi, ki, qa, ka, vt)


def _s5_kernel(ua_ref, ub_ref, m_ref, bc_ref, cc_ref, ar_ref, ai_ref, ya_ref, yb_ref, v_ref, xp_ref):
    rows = ua_ref.shape[0] // SSM_CHUNK
    u = jnp.concatenate([r[pl.ds(s, rows, stride=SSM_CHUNK), :].astype(BF16)
                         for s in range(SSM_CHUNK) for r in (ua_ref, ub_ref)], axis=-1)
    half = ar_ref.shape[-1]
    v_ref[...] = _bdot(u, bc_ref[...])
    ar = ar_ref[...]
    ai = ai_ref[...]

    def group(i, carry):
        xr, xi = carry
        base = pl.multiple_of(i * 8, 8)
        blk = v_ref[pl.ds(base, 8), :]
        prev = []
        for j in range(8):
            prev.append(jnp.concatenate([xr, xi], axis=-1))
            vr = blk[j:j + 1, :half]
            vi = blk[j:j + 1, half:]
            xr, xi = ar * xr - ai * xi + vr, ar * xi + ai * xr + vi
        xp_ref[pl.ds(base, 8), :] = jnp.concatenate(prev, axis=0)
        return xr, xi

    zero = jnp.zeros((1, half), F32)
    lax.fori_loop(0, rows // 8, group, (zero, zero))
    y = _bdot(u, m_ref[...]) + _bdot(xp_ref[...].astype(BF16), cc_ref[...])
    for s in range(SSM_CHUNK):
        ya_ref[pl.ds(s, rows, stride=SSM_CHUNK), :] = y[:, 2 * s * LANES:(2 * s + 1) * LANES]
        yb_ref[pl.ds(s, rows, stride=SSM_CHUNK), :] = y[:, (2 * s + 1) * LANES:(2 * s + 2) * LANES]


def _s5(ua, ub, m, bc, cc, ar, ai, b):
    n = ua.shape[0]
    l = n // b
    rows = l // SSM_CHUNK
    half = pl.BlockSpec((l, LANES), lambda i: (i, 0))
    out = jax.ShapeDtypeStruct((n, LANES), F32)
    return pl.pallas_call(
        _s5_kernel,
        grid=(b,),
        in_specs=[half, half] + [_resident(a.shape) for a in (m, bc, cc, ar, ai)],
        out_specs=[half, half],
        out_shape=[out, out],
        scratch_shapes=[pltpu.VMEM((rows, bc.shape[1]), F32), pltpu.VMEM((rows, bc.shape[1]), F32)],
        compiler_params=pltpu.CompilerParams(dimension_semantics=("parallel",),
                                             vmem_limit_bytes=VMEM_LIMIT),
        name="s5",
    )(ua, ub, m, bc, cc, ar, ai)


def _s5_operators(a_re, a_im, log_dt, b_re, b_im, c_re, c_im, d_skip):
    t = SSM_CHUNK
    g, p = a_re.shape
    hc = d_skip.shape[-1]
    lam = lax.complex(a_re.astype(F32), a_im.astype(F32))
    dt = jnp.exp(log_dt.astype(F32))[:, None]
    a_bar = jnp.exp(lam * dt)
    b_bar = ((a_bar - 1.0) / lam)[..., None] * lax.complex(b_re.astype(F32), b_im.astype(F32))
    c = lax.complex(c_re.astype(F32), c_im.astype(F32))
    tau = jnp.arange(t + 1, dtype=F32)
    apow = jnp.exp((lam * dt)[None] * tau[:, None, None])
    eye_g = jnp.eye(g, dtype=F32)
    kern = jnp.real(jnp.einsum('gop,tgp,gpi->tgio', c, apow[:t], b_bar))
    kern = kern.at[0].add(jnp.eye(hc, dtype=F32)[None] * d_skip.astype(F32)[:, :, None])
    s_idx = jnp.arange(t)[:, None]
    t_idx = jnp.arange(t)[None, :]
    diff = t_idx - s_idx
    kst = jnp.where((diff >= 0)[:, :, None, None, None], kern[jnp.clip(diff, 0, t - 1)], 0.0)
    m = jnp.einsum('stgio,gk->sgitko', kst, eye_g).reshape(t * g * hc, t * g * hc)
    bs = apow[:t][::-1][:, :, :, None] * b_bar[None]
    bcx = jnp.einsum('sgph,gk->sghkp', bs, eye_g.astype(bs.dtype)).reshape(t * g * hc, g * p)
    bc = jnp.concatenate([jnp.real(bcx), jnp.imag(bcx)], axis=1)
    ct = c[None] * apow[1:][:, :, None, :]
    ccx = jnp.einsum('tgop,gk->gptko', ct, eye_g.astype(ct.dtype)).reshape(g * p, t * g * hc)
    cc = jnp.concatenate([jnp.real(ccx), -jnp.imag(ccx)], axis=0)
    a_t = apow[t].reshape(1, g * p)
    return (m.astype(BF16), bc.astype(BF16), cc.astype(BF16),
            jnp.real(a_t).astype(F32), jnp.imag(a_t).astype(F32))


def _conv_kernel(h_ref, w_ref, b_ref, g_ref, beta_ref, o_ref, buf_ref):
    t = h_ref.shape[0]

    @pl.when(pl.program_id(1) == 0)
    def _():
        buf_ref[0:CONV_HALO, :] = jnp.zeros((CONV_HALO, buf_ref.shape[1]), F32)

    @pl.when(pl.program_id(1) > 0)
    def _():
        buf_ref[0:CONV_HALO, :] = buf_ref[t:t + CONV_HALO, :]

    buf_ref[CONV_HALO:, :] = h_ref[...].astype(F32)
    off = CONV_HALO - (CONV_K - 1)
    rb = min(CONV_ROWS, t)
    for r0 in range(0, t, rb):
        acc = jnp.zeros((rb, h_ref.shape[1]), F32)
        for j in range(CONV_K):
            acc = acc + w_ref[j:j + 1, :] * buf_ref[r0 + off + j:r0 + off + j + rb, :]
        y = acc + b_ref[...]
        mu = jnp.mean(y, axis=-1, keepdims=True)
        yc = y - mu
        var = jnp.mean(yc * yc, axis=-1, keepdims=True)
        z = yc * lax.rsqrt(var + EPS) * g_ref[...] + beta_ref[...]
        o_ref[r0:r0 + rb, :] = _silu(z).astype(BF16)


def _conv(hc, w, bias, g, beta, b, l):
    t = min(CONV_TILE, l)
    nt = l // t
    cw = hc.shape[1]
    return pl.pallas_call(
        _conv_kernel,
        grid=(b, nt),
        in_specs=[pl.BlockSpec((t, cw), lambda bi, i: (bi * nt + i, 0))]
                 + [_resident(a.shape) for a in (w, bias, g, beta)],
        out_specs=pl.BlockSpec((t, cw), lambda bi, i: (bi * nt + i, 0)),
        out_shape=jax.ShapeDtypeStruct(hc.shape, BF16),
        scratch_shapes=[pltpu.VMEM((t + CONV_HALO, cw), F32)],
        compiler_params=pltpu.CompilerParams(dimension_semantics=("parallel", "arbitrary"),
                                             vmem_limit_bytes=VMEM_LIMIT),
        name="conv",
    )(hc, w, bias, g, beta)


def _pack_halves(h):
    w = h.shape[-1] // 2
    bits = lax.bitcast_convert_type(h.astype(BF16).astype(F32), jnp.uint32)
    return bits[:, :w] | (bits[:, w:] >> 16)


def _unpack_halves(u):
    hi = lax.bitcast_convert_type(u & jnp.uint32(0xFFFF0000), F32)
    lo = lax.bitcast_convert_type(u << 16, F32)
    return jnp.concatenate([hi, lo], axis=-1).astype(BF16)


def _merge_kernel(x_ref, att_ref, ya_ref, yb_ref, conv_ref, gates_ref, wglu_ref, wa_ref, ws_ref, wc_ref, wo_ref,
                  gf_ref, xo_ref, h_ref, *, packed):
    d = x_ref.shape[-1]
    y = jnp.concatenate([ya_ref[...], yb_ref[...]], axis=-1)
    sw = y.shape[-1]
    yg = _bdot(_gelu_tanh(y).astype(BF16), wglu_ref[...])
    ssm = (yg[:, :sw] * _sigmoid(yg[:, sw:])).astype(BF16)
    merged = gates_ref[:, 0:d].astype(F32) * _bdot(att_ref[...], wa_ref[...])
    merged = merged + gates_ref[:, d:2 * d].astype(F32) * _bdot(ssm, ws_ref[...])
    merged = merged + gates_ref[:, 2 * d:3 * d].astype(F32) * _bdot(conv_ref[...], wc_ref[...])
    xn = x_ref[...] + _bdot(merged.astype(BF16), wo_ref[...])
    xo_ref[...] = xn
    hn = _rms(xn, gf_ref[...])
    h_ref[...] = _pack_halves(hn) if packed else hn.astype(BF16)


def _merge(x2, att, ya, yb, conv, gates, wglu, wa, ws, wc, wo, gf, packed):
    n, d = x2.shape
    tm = min(TOKEN_TILE, n)
    row = lambda w: pl.BlockSpec((tm, w), lambda i: (i, 0))
    hw, hdt = (d // 2, jnp.uint32) if packed else (d, BF16)
    return pl.pallas_call(
        functools.partial(_merge_kernel, packed=packed),
        grid=(n // tm,),
        in_specs=[row(d), row(att.shape[1]), row(LANES), row(LANES), row(conv.shape[1]), row(gates.shape[1])]
                 + [_resident(a.shape) for a in (wglu, wa, ws, wc, wo, gf)],
        out_specs=[row(d), row(hw)],
        out_shape=[jax.ShapeDtypeStruct((n, d), F32), jax.ShapeDtypeStruct((n, hw), hdt)],
        compiler_params=pltpu.CompilerParams(dimension_semantics=("parallel",),
                                             vmem_limit_bytes=VMEM_LIMIT),
        name="merge",
    )(x2, att, ya, yb, conv, gates, wglu, wa, ws, wc, wo, gf)


def _ffn_kernel(x_ref, h_ref, wg_ref, wu_ref, wd_ref, xo_ref, *, chunks):
    h = h_ref[...]
    f = wg_ref.shape[1]
    fc = f // chunks
    acc = x_ref[...]
    for j in range(chunks):
        sl = slice(j * fc, (j + 1) * fc)
        act = (_silu(_bdot(h, wg_ref[:, sl])) * _bdot(h, wu_ref[:, sl])).astype(BF16)
        acc = acc + _bdot(act, wd_ref[sl, :])
    xo_ref[...] = acc


def _ffn(x2, h, wg, wu, wd):
    n, d = x2.shape
    tm = min(TOKEN_TILE, n)
    row = pl.BlockSpec((tm, d), lambda i: (i, 0))
    f = wg.shape[1]
    chunks = 2 if f % (2 * LANES) == 0 else 1
    return pl.pallas_call(
        functools.partial(_ffn_kernel, chunks=chunks),
        grid=(n // tm,),
        in_specs=[row, row] + [_resident(a.shape) for a in (wg, wu, wd)],
        out_specs=row,
        out_shape=jax.ShapeDtypeStruct((n, d), F32),
        compiler_params=pltpu.CompilerParams(dimension_semantics=("parallel",),
                                             vmem_limit_bytes=VMEM_LIMIT),
        name="ffn",
    )(x2, h, wg, wu, wd)


META_E1, META_E2, META_P1, META_P2, META_R1, META_R2 = range(6)


def _router_kernel(x_ref, g_ref, wr_ref, br_ref, meta_ref, counts_ref):
    @pl.when(pl.program_id(0) == 0)
    def _():
        counts_ref[...] = jnp.zeros_like(counts_ref)

    h = _rms(x_ref[...], g_ref[...])
    logits = jnp.dot(h, wr_ref[...], preferred_element_type=F32, precision=HIGHEST) + br_ref[...]
    tm = logits.shape[0]
    lane = lax.broadcasted_iota(jnp.int32, logits.shape, 1)
    logits = jnp.where(lane < N_EXPERTS, logits, NEG_BIG)
    m1 = jnp.max(logits, axis=-1, keepdims=True)
    i1 = jnp.min(jnp.where(logits == m1, lane, LANES), axis=-1, keepdims=True)
    rest = jnp.where(lane == i1, NEG_BIG, logits)
    m2 = jnp.max(rest, axis=-1, keepdims=True)
    i2 = jnp.min(jnp.where(rest == m2, lane, LANES), axis=-1, keepdims=True)
    e = jnp.exp(m2 - m1)
    p1 = 1.0 / (1.0 + e)
    p2 = e / (1.0 + e)
    assigned = jnp.where((lane == i1) | (lane == i2), 1.0, 0.0)
    r = lax.broadcasted_iota(jnp.int32, (tm, tm), 0)
    c = lax.broadcasted_iota(jnp.int32, (tm, tm), 1)
    before = _bdot((c < r).astype(BF16), assigned.astype(BF16)) + counts_ref[...]
    rank1 = jnp.sum(jnp.where(lane == i1, before, 0.0), axis=-1, keepdims=True)
    rank2 = jnp.sum(jnp.where(lane == i2, before, 0.0), axis=-1, keepdims=True)
    meta = jnp.where(lane == META_E1, i1.astype(F32), 0.0)
    for idx, val in ((META_E2, i2.astype(F32)), (META_P1, p1), (META_P2, p2), (META_R1, rank1), (META_R2, rank2)):
        meta = jnp.where(lane == idx, val, meta)
    meta_ref[...] = meta
    counts_ref[...] += jnp.sum(assigned, axis=0, keepdims=True)


def _router(x2, g, wr, br):
    n, d = x2.shape
    tm = min(TOKEN_TILE, n)
    return pl.pallas_call(
        _router_kernel,
        grid=(n // tm,),
        in_specs=[pl.BlockSpec((tm, d), lambda i: (i, 0))] + [_resident(a.shape) for a in (g, wr, br)],
        out_specs=[pl.BlockSpec((tm, LANES), lambda i: (i, 0)), pl.BlockSpec((1, LANES), lambda i: (0, 0))],
        out_shape=[jax.ShapeDtypeStruct((n, LANES), F32), jax.ShapeDtypeStruct((1, LANES), F32)],
        compiler_params=pltpu.CompilerParams(dimension_semantics=("arbitrary",),
                                             vmem_limit_bytes=VMEM_LIMIT),
        name="router",
    )(x2, g, wr, br)


def _dispatch_plan(meta, counts, tm, n_tiles):
    cnt = counts[0, :N_EXPERTS].astype(jnp.int32)
    tiles = (cnt + tm - 1) // tm
    tile_end = jnp.cumsum(tiles)
    row_start = (tile_end - tiles) * tm
    t_idx = jnp.arange(n_tiles, dtype=jnp.int32)
    n_used = tile_end[-1]
    tile_expert = jnp.sum((t_idx[:, None] >= tile_end[None, :]).astype(jnp.int32), axis=1)
    last_expert = jnp.sum((n_used - 1 >= tile_end).astype(jnp.int32))
    tile_expert = jnp.where(t_idx < n_used, tile_expert, last_expert)
    experts = jnp.arange(N_EXPERTS, dtype=jnp.int32)

    def pos(e_lane, r_lane):
        e = meta[:, e_lane].astype(jnp.int32)
        start = jnp.sum(jnp.where(e[:, None] == experts[None, :], row_start[None, :], 0), axis=1)
        return start + meta[:, r_lane].astype(jnp.int32)

    pos2 = jnp.stack([pos(META_E1, META_R1), pos(META_E2, META_R2)], axis=1)
    return pos2, tile_expert, n_used.reshape(1)


def _dispatch_kernel(pos_ref, h_ref, init_ref, sorted_ref, sem):
    del init_ref
    tm = h_ref.shape[0]

    def row_copy(r, k):
        return pltpu.make_async_copy(h_ref.at[pl.ds(r, 1), :],
                                     sorted_ref.at[pl.ds(pos_ref[0, 0, 2 * r + k], 1), :], sem)

    def start(r, carry):
        row_copy(r, 0).start()
        row_copy(r, 1).start()
        return carry

    lax.fori_loop(0, tm, start, 0, unroll=DMA_UNROLL)
    tile_copy = pltpu.make_async_copy(h_ref, sorted_ref.at[pl.ds(0, tm), :], sem)
    tile_copy.wait()
    tile_copy.wait()


def _dispatch(hp, pos2, n_rows):
    n, w = hp.shape
    tm = min(MOE_TILE, n)
    pos_tiles = pos2.reshape(n // tm, 1, 2 * tm)
    return pl.pallas_call(
        _dispatch_kernel,
        grid=(n // tm,),
        in_specs=[pl.BlockSpec((1, 1, 2 * tm), lambda i: (i, 0, 0), memory_space=pltpu.SMEM),
                  pl.BlockSpec((tm, w), lambda i: (i, 0)),
                  pl.BlockSpec(memory_space=pl.ANY)],
        out_specs=pl.BlockSpec(memory_space=pl.ANY),
        out_shape=jax.ShapeDtypeStruct((n_rows, w), hp.dtype),
        scratch_shapes=[pltpu.SemaphoreType.DMA(())],
        input_output_aliases={2: 0},
        compiler_params=pltpu.CompilerParams(dimension_semantics=("arbitrary",),
                                             vmem_limit_bytes=VMEM_LIMIT),
        name="dispatch",
    )(pos_tiles, hp, jnp.zeros((n_rows, w), hp.dtype))


def _gffn_kernel(te_ref, nu_ref, rows_ref, wg_ref, wu_ref, wd_ref, y_ref, *, chunks):
    del te_ref

    @pl.when(pl.program_id(0) < nu_ref[0])
    def _():
        h = _unpack_halves(rows_ref[...])
        fc = wg_ref.shape[2] // chunks
        acc = None
        for j in range(chunks):
            sl = slice(j * fc, (j + 1) * fc)
            act = (_silu(_bdot(h, wg_ref[0, :, sl])) * _bdot(h, wu_ref[0, :, sl])).astype(BF16)
            part = _bdot(act, wd_ref[0, sl, :])
            acc = part if acc is None else acc + part
        y_ref[...] = acc

    @pl.when(pl.program_id(0) >= nu_ref[0])
    def _():
        y_ref[...] = jnp.zeros_like(y_ref)


def _gffn(rows, tile_expert, n_used, wg, wu, wd):
    n_rows, w = rows.shape
    ne, d, f = wg.shape
    tm = min(MOE_TILE, n_rows)
    chunks = 2 if f % (2 * LANES) == 0 else 1
    wspec = lambda shape: pl.BlockSpec(shape, lambda t, te, nu: (te[t], 0, 0), pipeline_mode=pl.Buffered(1))
    return pl.pallas_call(
        functools.partial(_gffn_kernel, chunks=chunks),
        grid_spec=pltpu.PrefetchScalarGridSpec(
            num_scalar_prefetch=2,
            grid=(n_rows // tm,),
            in_specs=[pl.BlockSpec((tm, w), lambda t, te, nu: (t, 0)),
                      wspec((1, d, f)), wspec((1, d, f)), wspec((1, f, d))],
            out_specs=pl.BlockSpec((tm, d), lambda t, te, nu: (t, 0)),
        ),
        out_shape=jax.ShapeDtypeStruct((n_rows, d), F32),
        compiler_params=pltpu.CompilerParams(dimension_semantics=("arbitrary",),
                                             vmem_limit_bytes=VMEM_LIMIT),
        name="grouped_ffn",
    )(tile_expert, n_used, rows, wg, wu, wd)


def _combine_kernel(pos_ref, x_ref, meta_ref, g_ref, y_ref, o_ref, buf1_ref, buf2_ref, sem, *, final):
    tm = x_ref.shape[0]
    bufs = (buf1_ref, buf2_ref)

    def row_copy(r, k):
        return pltpu.make_async_copy(y_ref.at[pl.ds(pos_ref[0, 0, 2 * r + k], 1), :],
                                     bufs[k].at[pl.ds(r, 1), :], sem)

    def start(r, carry):
        row_copy(r, 0).start()
        row_copy(r, 1).start()
        return carry

    lax.fori_loop(0, tm, start, 0, unroll=DMA_UNROLL)
    for buf in bufs:
        pltpu.make_async_copy(y_ref.at[pl.ds(0, tm), :], buf, sem).wait()
    meta = meta_ref[...]
    p1 = meta[:, META_P1:META_P1 + 1]
    p2 = meta[:, META_P2:META_P2 + 1]
    xn = x_ref[...] + p1 * buf1_ref[...] + p2 * buf2_ref[...]
    o_ref[...] = _rms(xn, g_ref[...]) if final else xn


def _combine(x2, meta, pos2, y, g, final):
    n, d = x2.shape
    tm = min(MOE_TILE, n)
    pos_tiles = pos2.reshape(n // tm, 1, 2 * tm)
    return pl.pallas_call(
        functools.partial(_combine_kernel, final=final),
        grid=(n // tm,),
        in_specs=[pl.BlockSpec((1, 1, 2 * tm), lambda i: (i, 0, 0), memory_space=pltpu.SMEM),
                  pl.BlockSpec((tm, d), lambda i: (i, 0)),
                  pl.BlockSpec((tm, LANES), lambda i: (i, 0)),
                  _resident(g.shape),
                  pl.BlockSpec(memory_space=pl.ANY)],
        out_specs=pl.BlockSpec((tm, d), lambda i: (i, 0)),
        out_shape=jax.ShapeDtypeStruct((n, d), F32),
        scratch_shapes=[pltpu.VMEM((tm, d), F32), pltpu.VMEM((tm, d), F32), pltpu.SemaphoreType.DMA(())],
        compiler_params=pltpu.CompilerParams(dimension_semantics=("arbitrary",),
                                             vmem_limit_bytes=VMEM_LIMIT),
        name="combine",
    )(pos_tiles, x2, meta, g, y)


def _moe(x2, hp, g_ffn_row, wr, br, wg, wu, wd, g_out, final):
    n, _ = x2.shape
    tm = min(MOE_TILE, n)
    n_tiles = 2 * n // tm + N_EXPERTS
    meta, counts = _router(x2, g_ffn_row, wr, br)
    pos2, tile_expert, n_used = _dispatch_plan(meta, counts, tm, n_tiles)
    rows = _dispatch(hp, pos2, n_tiles * tm)
    y = _gffn(rows, tile_expert, n_used, wg, wu, wd)
    return _combine(x2, meta, pos2, y, g_out, final)


def _final_kernel(x_ref, g_ref, o_ref):
    o_ref[...] = _rms(x_ref[...], g_ref[...])


def _final_norm(x2, g):
    n, d = x2.shape
    tm = min(TOKEN_TILE, n)
    row = pl.BlockSpec((tm, d), lambda i: (i, 0))
    return pl.pallas_call(
        _final_kernel,
        grid=(n // tm,),
        in_specs=[row, _resident(g.shape)],
        out_specs=row,
        out_shape=jax.ShapeDtypeStruct((n, d), F32),
        compiler_params=pltpu.CompilerParams(dimension_semantics=("parallel",),
                                             vmem_limit_bytes=VMEM_LIMIT),
        name="final_norm",
    )(x2, g)


def _pad_cols(a, width):
    return jnp.pad(a, ((0, 0), (0, width - a.shape[1])))


def kernel(x, g_mix, w_in, b_in, ssm_a_re, ssm_a_im, ssm_log_dt, ssm_b_re, ssm_b_im, ssm_c_re, ssm_c_im, ssm_d, ssm_w_glu, conv_w, conv_b, conv_ln_g, conv_ln_b, w_att_out, w_ssm_out, w_conv_out, w_o, g_ffn, ffn_w_gate, ffn_w_up, ffn_w_down, moe_w_router, moe_b_router, moe_w_gate, moe_w_up, moe_w_down, g_final):
    b, l, d = x.shape
    depth = w_in.shape[0]
    n = b * l
    aw = ATT_HEADS * ATT_HEAD_DIM
    sw = SSM_GROUPS * SSM_GROUP
    cw = conv_w.shape[-1]
    q_end, k_end, v_end = aw, 2 * aw, 3 * aw
    f_end = v_end + ATT_HEADS
    u_end = f_end + sw
    c_end = u_end + 2 * cw
    scale = ATT_HEAD_DIM ** -0.5
    col_scale = jnp.concatenate([jnp.full((aw,), scale, F32), jnp.ones((2 * aw,), F32)])
    row2 = lambda v: v.reshape(1, -1).astype(F32)

    x2 = x.reshape(n, d)
    for layer in range(depth):
        w = w_in[layer]
        bias = b_in[layer]
        wqkv = (w[:, :v_end] * col_scale).astype(BF16)
        bqkv = row2(bias[:v_end] * col_scale)
        wf = _pad_cols(w[:, v_end:f_end], LANES).astype(BF16)
        bf = _pad_cols(row2(bias[v_end:f_end]), LANES)
        qkv, logf, ua, ub, hc, gates = _inproj(
            x2, row2(g_mix[layer]), wqkv, bqkv, wf, bf,
            w[:, f_end:u_end].astype(BF16), row2(bias[f_end:u_end]),
            w[:, u_end:c_end].astype(BF16), row2(bias[u_end:c_end]),
            w[:, c_end:].astype(BF16), row2(bias[c_end:]))

        qa, ka, va = _foxprep(qkv, logf, b, l)
        att = _fox_attention(qa, ka, va).reshape(n, aw)

        m, bc, cc, ar, ai = _s5_operators(ssm_a_re[layer], ssm_a_im[layer], ssm_log_dt[layer],
                                          ssm_b_re[layer], ssm_b_im[layer], ssm_c_re[layer],
                                          ssm_c_im[layer], ssm_d[layer])
        ya, yb = _s5(ua, ub, m, bc, cc, ar, ai, b)

        wconv = jnp.pad(conv_w[layer].astype(F32), ((0, CONV_HALO - CONV_K), (0, 0)))
        conv = _conv(hc, wconv, row2(conv_b[layer]), row2(conv_ln_g[layer]), row2(conv_ln_b[layer]), b, l)

        is_moe = layer % 2 == 1
        last = layer == depth - 1
        x2, h = _merge(x2, att, ya, yb, conv, gates, ssm_w_glu[layer].astype(BF16),
                       w_att_out[layer].astype(BF16), w_ssm_out[layer].astype(BF16),
                       w_conv_out[layer].astype(BF16), w_o[layer].astype(BF16), row2(g_ffn[layer]), is_moe)

        i = layer // 2
        if is_moe:
            x2 = _moe(x2, h, row2(g_ffn[layer]), _pad_cols(moe_w_router[i].astype(F32), LANES),
                      _pad_cols(row2(moe_b_router[i]), LANES), moe_w_gate[i].astype(BF16),
                      moe_w_up[i].astype(BF16), moe_w_down[i].astype(BF16), row2(g_final), last)
        else:
            x2 = _ffn(x2, h, ffn_w_gate[i].astype(BF16), ffn_w_up[i].astype(BF16), ffn_w_down[i].astype(BF16))
            if last:
                x2 = _final_norm(x2, row2(g_final))
    return x2.reshape(b, l, d)
```

```python
import functools
import math

import jax
import jax.numpy as jnp
from jax import lax
from jax.experimental import pallas as pl
from jax.experimental.pallas import tpu as pltpu

F32 = jnp.float32
BF16 = jnp.bfloat16
HIGHEST = lax.Precision.HIGHEST

EPS = 1e-6
ATT_HEADS = 8
ATT_HEAD_DIM = 64
SSM_GROUPS = 16
SSM_GROUP = 16
SSM_STATE = 64
CONV_K = 31
N_EXPERTS = 8
LANES = 128
HEAD_PAD = 128
SSM_CHUNK = 8
CONV_HALO = 32
NEG_BIG = -1e30

TOKEN_TILE = 512
ATT_TILE = 512
CONV_TILE = 512
CONV_ROWS = 64
MOE_TILE = 512
DMA_UNROLL = 8
VMEM_LIMIT = 56 * 1024 * 1024


def _resident(shape):
    nd = len(shape)
    return pl.BlockSpec(shape, lambda *_: (0,) * nd, pipeline_mode=pl.Buffered(1))


def _sigmoid(x):
    return 1.0 / (1.0 + jnp.exp(-x))


def _silu(x):
    return x * _sigmoid(x)


def _gelu_tanh(x):
    c = math.sqrt(2.0 / math.pi)
    return 0.5 * x * (1.0 + jnp.tanh(c * (x + 0.044715 * (x * x * x))))


def _rms(x, g):
    return x * lax.rsqrt(jnp.mean(x * x, axis=-1, keepdims=True) + EPS) * g


def _bdot(a, b):
    return jnp.dot(a, b, preferred_element_type=F32)


def _split3(f):
    hi = f.astype(BF16).astype(F32)
    r = f - hi
    mid = r.astype(BF16).astype(F32)
    lo = (r - mid).astype(BF16).astype(F32)
    return hi, mid, lo


def _inproj_kernel(x_ref, g_ref, wq_ref, bq_ref, wk_ref, bk_ref, wvt_ref, bvt_ref, wf_ref, bf_ref, wu_ref, bu_ref,
                   wc_ref, bc_ref, wg_ref, bg_ref,
                   qa_ref, ka_ref, vt_ref, ua_ref, ub_ref, hc_ref, gates_ref, carry_ref, *, tiles_per_seq):
    tm = x_ref.shape[0]
    dh = ATT_HEAD_DIM

    @pl.when(pl.program_id(0) % tiles_per_seq == 0)
    def _():
        carry_ref[...] = jnp.zeros_like(carry_ref)

    h = _rms(x_ref[...], g_ref[...]).astype(BF16)

    fz = _bdot(h, wf_ref[...]) + bf_ref[...]
    logf = jnp.minimum(fz, 0.0) - jnp.log1p(jnp.exp(-jnp.abs(fz)))
    r = lax.broadcasted_iota(jnp.int32, (tm, tm), 0)
    c = lax.broadcasted_iota(jnp.int32, (tm, tm), 1)
    fcum = jnp.dot((c <= r).astype(F32), logf, preferred_element_type=F32, precision=HIGHEST) + carry_ref[...]
    carry_ref[...] = fcum[tm - 1:tm, :]

    lane = lax.broadcasted_iota(jnp.int32, (tm, HEAD_PAD), 1) - dh
    ones = jnp.where((lane >= 0) & (lane < 6), 1.0, 0.0)
    q = _bdot(h, wq_ref[...]) + bq_ref[...]
    k = _bdot(h, wk_ref[...]) + bk_ref[...]
    for hd in range(ATT_HEADS):
        blk = slice((hd // 2) * HEAD_PAD, (hd // 2 + 1) * HEAD_PAD)
        qh, kh = q[:, blk], k[:, blk]
        if hd % 2 == 1:
            qh = pltpu.roll(qh, dh, 1)
            kh = pltpu.roll(kh, dh, 1)
        hi, mid, lo = _split3(jnp.broadcast_to(fcum[:, hd:hd + 1], (tm, HEAD_PAD)))
        qx = jnp.where(lane == 0, hi, jnp.where(lane == 1, mid, jnp.where(lane == 2, lo, ones)))
        kx = jnp.where(lane == 3, -hi, jnp.where(lane == 4, -mid, jnp.where(lane == 5, -lo, ones)))
        qa_ref[0, hd] = jnp.where(lane < 0, qh, qx).astype(BF16)
        ka_ref[0, hd] = jnp.where(lane < 0, kh, kx).astype(BF16)

    vt = lax.dot_general(wvt_ref[...], h, (((1,), (1,)), ((), ())), preferred_element_type=F32) + bvt_ref[...]
    row = lax.broadcasted_iota(jnp.int32, (HEAD_PAD - dh, tm), 0)
    tail = jnp.where(row == 0, 1.0, 0.0).astype(BF16)
    for hd in range(ATT_HEADS):
        vt_ref[0, hd, 0:dh, :] = vt[hd * dh:(hd + 1) * dh, :].astype(BF16)
        vt_ref[0, hd, dh:HEAD_PAD, :] = tail

    u = _bdot(h, wu_ref[...]) + bu_ref[...]
    ua_ref[...] = u[:, :LANES]
    ub_ref[...] = u[:, LANES:]
    cz = _bdot(h, wc_ref[...]) + bc_ref[...]
    cw = cz.shape[-1] // 2
    hc_ref[...] = (cz[:, :cw] * _sigmoid(cz[:, cw:])).astype(BF16)
    d = x_ref.shape[-1]
    for j in range(gates_ref.shape[-1] // d):
        gz = _bdot(h, wg_ref[:, j * d:(j + 1) * d]) + bg_ref[:, j * d:(j + 1) * d]
        gates_ref[:, j * d:(j + 1) * d] = _sigmoid(gz).astype(BF16)


def _inproj(x2, b, weights):
    n, d = x2.shape
    l = n // b
    tm = min(TOKEN_TILE, l)
    tps = l // tm
    wc, wg = weights[11], weights[13]
    row = lambda w: pl.BlockSpec((tm, w), lambda i: (i, 0))
    head = jax.ShapeDtypeStruct((b, ATT_HEADS, l, HEAD_PAD), BF16)
    head_t = jax.ShapeDtypeStruct((b, ATT_HEADS, HEAD_PAD, l), BF16)
    hspec = pl.BlockSpec((1, ATT_HEADS, tm, HEAD_PAD), lambda i: (i // tps, 0, i % tps, 0))
    hspec_t = pl.BlockSpec((1, ATT_HEADS, HEAD_PAD, tm), lambda i: (i // tps, 0, 0, i % tps))
    flat = [(LANES, F32), (LANES, F32), (wc.shape[1] // 2, BF16), (wg.shape[1], BF16)]
    return pl.pallas_call(
        functools.partial(_inproj_kernel, tiles_per_seq=tps),
        grid=(n // tm,),
        in_specs=[row(d)] + [_resident(a.shape) for a in weights],
        out_specs=[hspec, hspec, hspec_t] + [row(w) for w, _ in flat],
        out_shape=[head, head, head_t] + [jax.ShapeDtypeStruct((n, w), dt) for w, dt in flat],
        scratch_shapes=[pltpu.VMEM((1, LANES), F32)],
        compiler_params=pltpu.CompilerParams(dimension_semantics=("arbitrary",),
                                             vmem_limit_bytes=VMEM_LIMIT),
        name="inproj",
    )(x2, *weights)


def _fox_kernel(qi_ref, ki_ref, qa_ref, ka_ref, vt_ref, o_ref, m_ref, acc_ref, s_ref):
    p_id = pl.program_id(1)
    qi = qi_ref[p_id]
    ki = ki_ref[p_id]
    t = qa_ref.shape[2]
    dh = ATT_HEAD_DIM

    @pl.when(ki == 0)
    def _():
        m_ref[...] = jnp.full_like(m_ref, NEG_BIG)
        acc_ref[...] = jnp.zeros_like(acc_ref)

    def scores(h, masked):
        st = lax.dot_general(ka_ref[0, h], qa_ref[0, h], (((1,), (1,)), ((), ())),
                             preferred_element_type=F32)
        if masked:
            kpos = lax.broadcasted_iota(jnp.int32, (t, t), 0)
            qpos = lax.broadcasted_iota(jnp.int32, (t, t), 1)
            st = jnp.where(kpos <= qpos, st, NEG_BIG)
        s_ref[h % 2] = st

    def step(masked):
        scores(0, masked)
        for h in range(ATT_HEADS):
            if h + 1 < ATT_HEADS:
                scores(h + 1, masked)
            st = s_ref[h % 2]
            m_prev = m_ref[h]
            m_new = jnp.maximum(m_prev, jnp.max(st, axis=0, keepdims=True))
            p = jnp.exp(st - m_new).astype(BF16)
            acc_ref[h] = jnp.exp(m_prev - m_new) * acc_ref[h] + _bdot(vt_ref[0, h], p)
            m_ref[h] = m_new

    @pl.when(ki < qi)
    def _():
        step(False)

    @pl.when(ki == qi)
    def _():
        step(True)
        for h in range(ATT_HEADS):
            a = acc_ref[h]
            o = (a / a[dh:dh + 1, :]).T
            o_ref[0, :, h * dh:(h + 1) * dh] = o[:, :dh].astype(BF16)


def _fox_attention(qa, ka, vt):
    b, nh, l, _ = qa.shape
    t = min(ATT_TILE, l)
    nt = l // t
    pairs = [(q, k) for q in range(nt) for k in range(q + 1)]
    qi = jnp.asarray([p[0] for p in pairs], jnp.int32)
    ki = jnp.asarray([p[1] for p in pairs], jnp.int32)
    qspec = pl.BlockSpec((1, nh, t, HEAD_PAD), lambda bi, p, qi, ki: (bi, 0, qi[p], 0))
    kspec = pl.BlockSpec((1, nh, t, HEAD_PAD), lambda bi, p, qi, ki: (bi, 0, ki[p], 0))
    vspec = pl.BlockSpec((1, nh, HEAD_PAD, t), lambda bi, p, qi, ki: (bi, 0, 0, ki[p]))
    return pl.pallas_call(
        _fox_kernel,
        grid_spec=pltpu.PrefetchScalarGridSpec(
            num_scalar_prefetch=2,
            grid=(b, len(pairs)),
            in_specs=[qspec, kspec, vspec],
            out_specs=pl.BlockSpec((1, t, nh * ATT_HEAD_DIM), lambda bi, p, qi, ki: (bi, qi[p], 0)),
            scratch_shapes=[pltpu.VMEM((nh, 1, t), F32), pltpu.VMEM((nh, HEAD_PAD, t), F32),
                            pltpu.VMEM((2, t, t), F32)],
        ),
        out_shape=jax.ShapeDtypeStruct((b, l, nh * ATT_HEAD_DIM), BF16),
        compiler_params=pltpu.CompilerParams(dimension_semantics=("parallel", "arbitrary"),
                                             vmem_limit_bytes=VMEM_LIMIT),
        name="fox_attention",
    )(qi, ki, qa, ka, vt)


def _s5_kernel(ua_ref, ub_ref, m_ref, bc_ref, cc_ref, ar_ref, ai_ref, ya_ref, yb_ref, v_ref, xp_ref):
    rows = ua_ref.shape[0] // SSM_CHUNK
    u = jnp.concatenate([r[pl.ds(s, rows, stride=SSM_CHUNK), :].astype(BF16)
                         for s in range(SSM_CHUNK) for r in (ua_ref, ub_ref)], axis=-1)
    half = ar_ref.shape[-1]
    v_ref[...] = _bdot(u, bc_ref[...])
    ar = ar_ref[...]
    ai = ai_ref[...]

    def group(i, carry):
        xr, xi = carry
        base = pl.multiple_of(i * 8, 8)
        blk = v_ref[pl.ds(base, 8), :]
        prev = []
        for j in range(8):
            prev.append(jnp.concatenate([xr, xi], axis=-1))
            vr = blk[j:j + 1, :half]
            vi = blk[j:j + 1, half:]
            xr, xi = ar * xr - ai * xi + vr, ar * xi + ai * xr + vi
        xp_ref[pl.ds(base, 8), :] = jnp.concatenate(prev, axis=0)
        return xr, xi

    zero = jnp.zeros((1, half), F32)
    lax.fori_loop(0, rows // 8, group, (zero, zero))
    y = _bdot(u, m_ref[...]) + _bdot(xp_ref[...].astype(BF16), cc_ref[...])
    for s in range(SSM_CHUNK):
        ya_ref[pl.ds(s, rows, stride=SSM_CHUNK), :] = y[:, 2 * s * LANES:(2 * s + 1) * LANES]
        yb_ref[pl.ds(s, rows, stride=SSM_CHUNK), :] = y[:, (2 * s + 1) * LANES:(2 * s + 2) * LANES]


def _s5(ua, ub, m, bc, cc, ar, ai, b):
    n = ua.shape[0]
    l = n // b
    rows = l // SSM_CHUNK
    half = pl.BlockSpec((l, LANES), lambda i: (i, 0))
    out = jax.ShapeDtypeStruct((n, LANES), F32)
    return pl.pallas_call(
        _s5_kernel,
        grid=(b,),
        in_specs=[half, half] + [_resident(a.shape) for a in (m, bc, cc, ar, ai)],
        out_specs=[half, half],
        out_shape=[out, out],
        scratch_shapes=[pltpu.VMEM((rows, bc.shape[1]), F32), pltpu.VMEM((rows, bc.shape[1]), F32)],
        compiler_params=pltpu.CompilerParams(dimension_semantics=("parallel",),
                                             vmem_limit_bytes=VMEM_LIMIT),
        name="s5",
    )(ua, ub, m, bc, cc, ar, ai)


def _s5_operators(a_re, a_im, log_dt, b_re, b_im, c_re, c_im, d_skip):
    t = SSM_CHUNK
    g, p = a_re.shape
    hc = d_skip.shape[-1]
    lam = lax.complex(a_re.astype(F32), a_im.astype(F32))
    dt = jnp.exp(log_dt.astype(F32))[:, None]
    a_bar = jnp.exp(lam * dt)
    b_bar = ((a_bar - 1.0) / lam)[..., None] * lax.complex(b_re.astype(F32), b_im.astype(F32))
    c = lax.complex(c_re.astype(F32), c_im.astype(F32))
    tau = jnp.arange(t + 1, dtype=F32)
    apow = jnp.exp((lam * dt)[None] * tau[:, None, None])
    eye_g = jnp.eye(g, dtype=F32)
    kern = jnp.real(jnp.einsum('gop,tgp,gpi->tgio', c, apow[:t], b_bar))
    kern = kern.at[0].add(jnp.eye(hc, dtype=F32)[None] * d_skip.astype(F32)[:, :, None])
    kb = jnp.einsum('tgio,gk->tgiko', kern, eye_g).reshape(t, g * hc, g * hc).astype(BF16)
    zero = jnp.zeros((g * hc, g * hc), BF16)
    m = jnp.concatenate([jnp.concatenate([kb[tt - ss] if tt >= ss else zero for tt in range(t)], axis=1)
                         for ss in range(t)], axis=0)
    bs = apow[:t][::-1][:, :, :, None] * b_bar[None]
    bcx = jnp.einsum('sgph,gk->sghkp', bs, eye_g.astype(bs.dtype)).reshape(t * g * hc, g * p)
    bc = jnp.concatenate([jnp.real(bcx), jnp.imag(bcx)], axis=1)
    ct = c[None] * apow[1:][:, :, None, :]
    ccx = jnp.einsum('tgop,gk->gptko', ct, eye_g.astype(ct.dtype)).reshape(g * p, t * g * hc)
    cc = jnp.concatenate([jnp.real(ccx), -jnp.imag(ccx)], axis=0)
    a_t = apow[t].reshape(1, g * p)
    return (m, bc.astype(BF16), cc.astype(BF16),
            jnp.real(a_t).astype(F32), jnp.imag(a_t).astype(F32))


def _conv_kernel(h_ref, w_ref, b_ref, g_ref, beta_ref, o_ref, sh_ref):
    t = h_ref.shape[0]
    cw = h_ref.shape[1]

    @pl.when(pl.program_id(1) == 0)
    def _():
        sh_ref[:, 0:CONV_HALO + 16, :] = jnp.zeros((8, CONV_HALO + 16, cw), F32)

    @pl.when(pl.program_id(1) > 0)
    def _():
        sh_ref[:, 0:CONV_HALO + 16, :] = sh_ref[:, t:t + CONV_HALO + 16, :]

    hv = h_ref[...].astype(F32)
    for r in range(8):
        sh_ref[r, CONV_HALO + 8 + r:CONV_HALO + 8 + r + t, :] = hv
    rb = min(CONV_ROWS, t)
    for r0 in range(0, t, rb):
        acc = jnp.zeros((rb, cw), F32)
        for j in range(CONV_K):
            delay = CONV_K - 1 - j
            base = CONV_HALO + 8 + r0 - 8 * (delay // 8)
            acc = acc + w_ref[j:j + 1, :] * sh_ref[delay % 8, base:base + rb, :]
        y = acc + b_ref[...]
        mu = jnp.mean(y, axis=-1, keepdims=True)
        yc = y - mu
        var = jnp.mean(yc * yc, axis=-1, keepdims=True)
        z = yc * lax.rsqrt(var + EPS) * g_ref[...] + beta_ref[...]
        o_ref[r0:r0 + rb, :] = _silu(z).astype(BF16)


def _conv(hc, w, bias, g, beta, b, l):
    t = min(CONV_TILE, l)
    nt = l // t
    cw = hc.shape[1]
    return pl.pallas_call(
        _conv_kernel,
        grid=(b, nt),
        in_specs=[pl.BlockSpec((t, cw), lambda bi, i: (bi * nt + i, 0))]
                 + [_resident(a.shape) for a in (w, bias, g, beta)],
        out_specs=pl.BlockSpec((t, cw), lambda bi, i: (bi * nt + i, 0)),
        out_shape=jax.ShapeDtypeStruct(hc.shape, BF16),
        scratch_shapes=[pltpu.VMEM((8, t + CONV_HALO + 16, cw), F32)],
        compiler_params=pltpu.CompilerParams(dimension_semantics=("parallel", "arbitrary"),
                                             vmem_limit_bytes=VMEM_LIMIT),
        name="conv",
    )(hc, w, bias, g, beta)


def _pack_halves(h):
    w = h.shape[-1] // 2
    bits = lax.bitcast_convert_type(h.astype(BF16).astype(F32), jnp.uint32)
    return bits[:, :w] | (bits[:, w:] >> 16)


def _unpack_halves(u):
    hi = lax.bitcast_convert_type(u & jnp.uint32(0xFFFF0000), F32)
    lo = lax.bitcast_convert_type(u << 16, F32)
    return jnp.concatenate([hi, lo], axis=-1).astype(BF16)


def _merge_kernel(x_ref, att_ref, ya_ref, yb_ref, conv_ref, gates_ref, wglu_ref, wa_ref, ws_ref, wc_ref, wo_ref,
                  gf_ref, xo_ref, h_ref, *, packed):
    d = x_ref.shape[-1]
    y = jnp.concatenate([ya_ref[...], yb_ref[...]], axis=-1)
    sw = y.shape[-1]
    yg = _bdot(_gelu_tanh(y).astype(BF16), wglu_ref[...])
    ssm = (yg[:, :sw] * _sigmoid(yg[:, sw:])).astype(BF16)
    merged = gates_ref[:, 0:d].astype(F32) * _bdot(att_ref[...], wa_ref[...])
    merged = merged + gates_ref[:, d:2 * d].astype(F32) * _bdot(ssm, ws_ref[...])
    merged = merged + gates_ref[:, 2 * d:3 * d].astype(F32) * _bdot(conv_ref[...], wc_ref[...])
    xn = x_ref[...] + _bdot(merged.astype(BF16), wo_ref[...])
    xo_ref[...] = xn
    hn = _rms(xn, gf_ref[...])
    h_ref[...] = _pack_halves(hn) if packed else hn.astype(BF16)


def _merge(x2, att, ya, yb, conv, gates, wglu, wa, ws, wc, wo, gf, packed):
    n, d = x2.shape
    tm = min(TOKEN_TILE, n)
    row = lambda w: pl.BlockSpec((tm, w), lambda i: (i, 0))
    hw, hdt = (d // 2, jnp.uint32) if packed else (d, BF16)
    return pl.pallas_call(
        functools.partial(_merge_kernel, packed=packed),
        grid=(n // tm,),
        in_specs=[row(d), row(att.shape[1]), row(LANES), row(LANES), row(conv.shape[1]), row(gates.shape[1])]
                 + [_resident(a.shape) for a in (wglu, wa, ws, wc, wo, gf)],
        out_specs=[row(d), row(hw)],
        out_shape=[jax.ShapeDtypeStruct((n, d), F32), jax.ShapeDtypeStruct((n, hw), hdt)],
        compiler_params=pltpu.CompilerParams(dimension_semantics=("parallel",),
                                             vmem_limit_bytes=VMEM_LIMIT),
        name="merge",
    )(x2, att, ya, yb, conv, gates, wglu, wa, ws, wc, wo, gf)


def _ffn_kernel(x_ref, h_ref, wg_ref, wu_ref, wd_ref, xo_ref, *, chunks):
    h = h_ref[...]
    f = wg_ref.shape[1]
    fc = f // chunks
    acc = x_ref[...]
    for j in range(chunks):
        sl = slice(j * fc, (j + 1) * fc)
        act = (_silu(_bdot(h, wg_ref[:, sl])) * _bdot(h, wu_ref[:, sl])).astype(BF16)
        acc = acc + _bdot(act, wd_ref[sl, :])
    xo_ref[...] = acc


def _ffn(x2, h, wg, wu, wd):
    n, d = x2.shape
    tm = min(TOKEN_TILE, n)
    row = pl.BlockSpec((tm, d), lambda i: (i, 0))
    f = wg.shape[1]
    chunks = 2 if f % (2 * LANES) == 0 else 1
    return pl.pallas_call(
        functools.partial(_ffn_kernel, chunks=chunks),
        grid=(n // tm,),
        in_specs=[row, row] + [_resident(a.shape) for a in (wg, wu, wd)],
        out_specs=row,
        out_shape=jax.ShapeDtypeStruct((n, d), F32),
        compiler_params=pltpu.CompilerParams(dimension_semantics=("parallel",),
                                             vmem_limit_bytes=VMEM_LIMIT),
        name="ffn",
    )(x2, h, wg, wu, wd)


META_E1, META_E2, META_P1, META_P2, META_R1, META_R2 = range(6)


def _router_kernel(x_ref, g_ref, wr_ref, br_ref, meta_ref, counts_ref):
    @pl.when(pl.program_id(0) == 0)
    def _():
        counts_ref[...] = jnp.zeros_like(counts_ref)

    h = _rms(x_ref[...], g_ref[...])
    logits = jnp.dot(h, wr_ref[...], preferred_element_type=F32, precision=HIGHEST) + br_ref[...]
    tm = logits.shape[0]
    lane = lax.broadcasted_iota(jnp.int32, logits.shape, 1)
    logits = jnp.where(lane < N_EXPERTS, logits, NEG_BIG)
    m1 = jnp.max(logits, axis=-1, keepdims=True)
    i1 = jnp.min(jnp.where(logits == m1, lane, LANES), axis=-1, keepdims=True)
    rest = jnp.where(lane == i1, NEG_BIG, logits)
    m2 = jnp.max(rest, axis=-1, keepdims=True)
    i2 = jnp.min(jnp.where(rest == m2, lane, LANES), axis=-1, keepdims=True)
    e = jnp.exp(m2 - m1)
    p1 = 1.0 / (1.0 + e)
    p2 = e / (1.0 + e)
    assigned = jnp.where((lane == i1) | (lane == i2), 1.0, 0.0)
    r = lax.broadcasted_iota(jnp.int32, (tm, tm), 0)
    c = lax.broadcasted_iota(jnp.int32, (tm, tm), 1)
    before = _bdot((c < r).astype(BF16), assigned.astype(BF16)) + counts_ref[...]
    rank1 = jnp.sum(jnp.where(lane == i1, before, 0.0), axis=-1, keepdims=True)
    rank2 = jnp.sum(jnp.where(lane == i2, before, 0.0), axis=-1, keepdims=True)
    meta = jnp.where(lane == META_E1, i1.astype(F32), 0.0)
    for idx, val in ((META_E2, i2.astype(F32)), (META_P1, p1), (META_P2, p2), (META_R1, rank1), (META_R2, rank2)):
        meta = jnp.where(lane == idx, val, meta)
    meta_ref[...] = meta
    counts_ref[...] += jnp.sum(assigned, axis=0, keepdims=True)


def _router(x2, g, wr, br):
    n, d = x2.shape
    tm = min(TOKEN_TILE, n)
    return pl.pallas_call(
        _router_kernel,
        grid=(n // tm,),
        in_specs=[pl.BlockSpec((tm, d), lambda i: (i, 0))] + [_resident(a.shape) for a in (g, wr, br)],
        out_specs=[pl.BlockSpec((tm, LANES), lambda i: (i, 0)), pl.BlockSpec((1, LANES), lambda i: (0, 0))],
        out_shape=[jax.ShapeDtypeStruct((n, LANES), F32), jax.ShapeDtypeStruct((1, LANES), F32)],
        compiler_params=pltpu.CompilerParams(dimension_semantics=("arbitrary",),
                                             vmem_limit_bytes=VMEM_LIMIT),
        name="router",
    )(x2, g, wr, br)


def _dispatch_plan(meta, counts, tm, n_tiles):
    cnt = counts[0, :N_EXPERTS].astype(jnp.int32)
    tiles = (cnt + tm - 1) // tm
    tile_end = jnp.cumsum(tiles)
    row_start = (tile_end - tiles) * tm
    t_idx = jnp.arange(n_tiles, dtype=jnp.int32)
    n_used = tile_end[-1]
    tile_expert = jnp.sum((t_idx[:, None] >= tile_end[None, :]).astype(jnp.int32), axis=1)
    last_expert = jnp.sum((n_used - 1 >= tile_end).astype(jnp.int32))
    tile_expert = jnp.where(t_idx < n_used, tile_expert, last_expert)
    experts = jnp.arange(N_EXPERTS, dtype=jnp.int32)

    def pos(e_lane, r_lane):
        e = meta[:, e_lane].astype(jnp.int32)
        start = jnp.sum(jnp.where(e[:, None] == experts[None, :], row_start[None, :], 0), axis=1)
        return start + meta[:, r_lane].astype(jnp.int32)

    pos2 = jnp.stack([pos(META_E1, META_R1), pos(META_E2, META_R2)], axis=1)
    return pos2, tile_expert, n_used.reshape(1)


def _dispatch_kernel(pos_ref, h_ref, init_ref, sorted_ref, sem):
    del init_ref
    tm = h_ref.shape[0]

    def row_copy(r, k):
        return pltpu.make_async_copy(h_ref.at[pl.ds(r, 1), :],
                                     sorted_ref.at[pl.ds(pos_ref[0, 0, 2 * r + k], 1), :], sem)

    def start(r, carry):
        row_copy(r, 0).start()
        row_copy(r, 1).start()
        return carry

    lax.fori_loop(0, tm, start, 0, unroll=DMA_UNROLL)
    tile_copy = pltpu.make_async_copy(h_ref, sorted_ref.at[pl.ds(0, tm), :], sem)
    tile_copy.wait()
    tile_copy.wait()


def _dispatch(hp, pos2, n_rows):
    n, w = hp.shape
    tm = min(MOE_TILE, n)
    pos_tiles = pos2.reshape(n // tm, 1, 2 * tm)
    return pl.pallas_call(
        _dispatch_kernel,
        grid=(n // tm,),
        in_specs=[pl.BlockSpec((1, 1, 2 * tm), lambda i: (i, 0, 0), memory_space=pltpu.SMEM),
                  pl.BlockSpec((tm, w), lambda i: (i, 0)),
                  pl.BlockSpec(memory_space=pl.ANY)],
        out_specs=pl.BlockSpec(memory_space=pl.ANY),
        out_shape=jax.ShapeDtypeStruct((n_rows, w), hp.dtype),
        scratch_shapes=[pltpu.SemaphoreType.DMA(())],
        input_output_aliases={2: 0},
        compiler_params=pltpu.CompilerParams(dimension_semantics=("arbitrary",),
                                             vmem_limit_bytes=VMEM_LIMIT),
        name="dispatch",
    )(pos_tiles, hp, jnp.zeros((n_rows, w), hp.dtype))


def _gffn_kernel(te_ref, nu_ref, rows_ref, wg_ref, wu_ref, wd_ref, y_ref, *, chunks):
    del te_ref

    @pl.when(pl.program_id(0) < nu_ref[0])
    def _():
        h = _unpack_halves(rows_ref[...])
        fc = wg_ref.shape[2] // chunks
        acc = None
        for j in range(chunks):
            sl = slice(j * fc, (j + 1) * fc)
            act = (_silu(_bdot(h, wg_ref[0, :, sl])) * _bdot(h, wu_ref[0, :, sl])).astype(BF16)
            part = _bdot(act, wd_ref[0, sl, :])
            acc = part if acc is None else acc + part
        y_ref[...] = acc

    @pl.when(pl.program_id(0) >= nu_ref[0])
    def _():
        y_ref[...] = jnp.zeros_like(y_ref)


def _gffn(rows, tile_expert, n_used, wg, wu, wd):
    n_rows, w = rows.shape
    ne, d, f = wg.shape
    tm = min(MOE_TILE, n_rows)
    chunks = 2 if f % (2 * LANES) == 0 else 1
    wspec = lambda shape: pl.BlockSpec(shape, lambda t, te, nu: (te[t], 0, 0), pipeline_mode=pl.Buffered(1))
    return pl.pallas_call(
        functools.partial(_gffn_kernel, chunks=chunks),
        grid_spec=pltpu.PrefetchScalarGridSpec(
            num_scalar_prefetch=2,
            grid=(n_rows // tm,),
            in_specs=[pl.BlockSpec((tm, w), lambda t, te, nu: (t, 0)),
                      wspec((1, d, f)), wspec((1, d, f)), wspec((1, f, d))],
            out_specs=pl.BlockSpec((tm, d), lambda t, te, nu: (t, 0)),
        ),
        out_shape=jax.ShapeDtypeStruct((n_rows, d), F32),
        compiler_params=pltpu.CompilerParams(dimension_semantics=("arbitrary",),
                                             vmem_limit_bytes=VMEM_LIMIT),
        name="grouped_ffn",
    )(tile_expert, n_used, rows, wg, wu, wd)


def _combine_kernel(pos_ref, x_ref, meta_ref, g_ref, y_ref, o_ref, buf1_ref, buf2_ref, sem, *, final):
    tm = x_ref.shape[0]
    bufs = (buf1_ref, buf2_ref)

    def row_copy(r, k):
        return pltpu.make_async_copy(y_ref.at[pl.ds(pos_ref[0, 0, 2 * r + k], 1), :],
                                     bufs[k].at[pl.ds(r, 1), :], sem)

    def start(r, carry):
        row_copy(r, 0).start()
        row_copy(r, 1).start()
        return carry

    lax.fori_loop(0, tm, start, 0, unroll=DMA_UNROLL)
    for buf in bufs:
        pltpu.make_async_copy(y_ref.at[pl.ds(0, tm), :], buf, sem).wait()
    meta = meta_ref[...]
    p1 = meta[:, META_P1:META_P1 + 1]
    p2 = meta[:, META_P2:META_P2 + 1]
    xn = x_ref[...] + p1 * buf1_ref[...] + p2 * buf2_ref[...]
    o_ref[...] = _rms(xn, g_ref[...]) if final else xn


def _combine(x2, meta, pos2, y, g, final):
    n, d = x2.shape
    tm = min(MOE_TILE, n)
    pos_tiles = pos2.reshape(n // tm, 1, 2 * tm)
    return pl.pallas_call(
        functools.partial(_combine_kernel, final=final),
        grid=(n // tm,),
        in_specs=[pl.BlockSpec((1, 1, 2 * tm), lambda i: (i, 0, 0), memory_space=pltpu.SMEM),
                  pl.BlockSpec((tm, d), lambda i: (i, 0)),
                  pl.BlockSpec((tm, LANES), lambda i: (i, 0)),
                  _resident(g.shape),
                  pl.BlockSpec(memory_space=pl.ANY)],
        out_specs=pl.BlockSpec((tm, d), lambda i: (i, 0)),
        out_shape=jax.ShapeDtypeStruct((n, d), F32),
        scratch_shapes=[pltpu.VMEM((tm, d), F32), pltpu.VMEM((tm, d), F32), pltpu.SemaphoreType.DMA(())],
        compiler_params=pltpu.CompilerParams(dimension_semantics=("arbitrary",),
                                             vmem_limit_bytes=VMEM_LIMIT),
        name="combine",
    )(pos_tiles, x2, meta, g, y)


def _moe(x2, hp, g_ffn_row, wr, br, wg, wu, wd, g_out, final):
    n, _ = x2.shape
    tm = min(MOE_TILE, n)
    n_tiles = 2 * n // tm + N_EXPERTS
    meta, counts = _router(x2, g_ffn_row, wr, br)
    pos2, tile_expert, n_used = _dispatch_plan(meta, counts, tm, n_tiles)
    rows = _dispatch(hp, pos2, n_tiles * tm)
    y = _gffn(rows, tile_expert, n_used, wg, wu, wd)
    return _combine(x2, meta, pos2, y, g_out, final)


def _final_kernel(x_ref, g_ref, o_ref):
    o_ref[...] = _rms(x_ref[...], g_ref[...])


def _final_norm(x2, g):
    n, d = x2.shape
    tm = min(TOKEN_TILE, n)
    row = pl.BlockSpec((tm, d), lambda i: (i, 0))
    return pl.pallas_call(
        _final_kernel,
        grid=(n // tm,),
        in_specs=[row, _resident(g.shape)],
        out_specs=row,
        out_shape=jax.ShapeDtypeStruct((n, d), F32),
        compiler_params=pltpu.CompilerParams(dimension_semantics=("parallel",),
                                             vmem_limit_bytes=VMEM_LIMIT),
        name="final_norm",
    )(x2, g)


def _pad_cols(a, width):
    return jnp.pad(a, ((0, 0), (0, width - a.shape[1])))


def kernel(x, g_mix, w_in, b_in, ssm_a_re, ssm_a_im, ssm_log_dt, ssm_b_re, ssm_b_im, ssm_c_re, ssm_c_im, ssm_d, ssm_w_glu, conv_w, conv_b, conv_ln_g, conv_ln_b, w_att_out, w_ssm_out, w_conv_out, w_o, g_ffn, ffn_w_gate, ffn_w_up, ffn_w_down, moe_w_router, moe_b_router, moe_w_gate, moe_w_up, moe_w_down, g_final):
    b, l, d = x.shape
    depth = w_in.shape[0]
    n = b * l
    aw = ATT_HEADS * ATT_HEAD_DIM
    sw = SSM_GROUPS * SSM_GROUP
    cw = conv_w.shape[-1]
    q_end, k_end, v_end = aw, 2 * aw, 3 * aw
    f_end = v_end + ATT_HEADS
    u_end = f_end + sw
    c_end = u_end + 2 * cw
    scale = ATT_HEAD_DIM ** -0.5
    row2 = lambda v: v.reshape(1, -1).astype(F32)

    x2 = x.reshape(n, d)
    for layer in range(depth):
        w = w_in[layer]
        bias = b_in[layer]
        pad = lambda v: _pad_cols(v, LANES)
        col = lambda v: v.reshape(-1, 1).astype(F32)
        weights = (row2(g_mix[layer]),
                   (w[:, :q_end] * scale).astype(BF16), row2(bias[:q_end] * scale),
                   w[:, q_end:k_end].astype(BF16), row2(bias[q_end:k_end]),
                   w[:, k_end:v_end].T.astype(BF16), col(bias[k_end:v_end]),
                   pad(w[:, v_end:f_end]).astype(BF16), pad(row2(bias[v_end:f_end])),
                   w[:, f_end:u_end].astype(BF16), row2(bias[f_end:u_end]),
                   w[:, u_end:c_end].astype(BF16), row2(bias[u_end:c_end]),
                   w[:, c_end:].astype(BF16), row2(bias[c_end:]))
        qa, ka, vt, ua, ub, hc, gates = _inproj(x2, b, weights)
        att = _fox_attention(qa, ka, vt).reshape(n, aw)

        m, bc, cc, ar, ai = _s5_operators(ssm_a_re[layer], ssm_a_im[layer], ssm_log_dt[layer],
                                          ssm_b_re[layer], ssm_b_im[layer], ssm_c_re[layer],
                                          ssm_c_im[layer], ssm_d[layer])
        ya, yb = _s5(ua, ub, m, bc, cc, ar, ai, b)

        wconv = jnp.pad(conv_w[layer].astype(F32), ((0, CONV_HALO - CONV_K), (0, 0)))
        conv = _conv(hc, wconv, row2(conv_b[layer]), row2(conv_ln_g[layer]), row2(conv_ln_b[layer]), b, l)

        is_moe = layer % 2 == 1
        last = layer == depth - 1
        x2, h = _merge(x2, att, ya, yb, conv, gates, ssm_w_glu[layer].astype(BF16),
                       w_att_out[layer].astype(BF16), w_ssm_out[layer].astype(BF16),
                       w_conv_out[layer].astype(BF16), w_o[layer].astype(BF16), row2(g_ffn[layer]), is_moe)

        i = layer // 2
        if is_moe:
            x2 = _moe(x2, h, row2(g_ffn[layer]), _pad_cols(moe_w_router[i].astype(F32), LANES),
                      _pad_cols(row2(moe_b_router[i]), LANES), moe_w_gate[i].astype(BF16),
                      moe_w_up[i].astype(BF16), moe_w_down[i].astype(BF16), row2(g_final), last)
        else:
            x2 = _ffn(x2, h, ffn_w_gate[i].astype(BF16), ffn_w_up[i].astype(BF16), ffn_w_down[i].astype(BF16))
            if last:
                x2 = _final_norm(x2, row2(g_final))
    return x2.reshape(b, l, d)
```

```python
import functools
import math

import jax
import jax.numpy as jnp
from jax import lax
from jax.experimental import pallas as pl
from jax.experimental.pallas import tpu as pltpu

F32 = jnp.float32
BF16 = jnp.bfloat16
HIGHEST = lax.Precision.HIGHEST

EPS = 1e-6
ATT_HEADS = 8
ATT_HEAD_DIM = 64
SSM_GROUPS = 16
SSM_GROUP = 16
SSM_STATE = 64
CONV_K = 31
N_EXPERTS = 8
LANES = 128
HEAD_PAD = 128
SSM_CHUNK = 8
CONV_HALO = 32
NEG_BIG = -1e30

TOKEN_TILE = 512
ATT_TILE = 512
CONV_TILE = 512
CONV_ROWS = 64
MOE_TILE = 512
DMA_UNROLL = 8
VMEM_LIMIT = 56 * 1024 * 1024


def _resident(shape):
    nd = len(shape)
    return pl.BlockSpec(shape, lambda *_: (0,) * nd, pipeline_mode=pl.Buffered(1))


def _sigmoid(x):
    return 1.0 / (1.0 + jnp.exp(-x))


def _silu(x):
    return x * _sigmoid(x)


def _gelu_tanh(x):
    c = math.sqrt(2.0 / math.pi)
    return 0.5 * x * (1.0 + jnp.tanh(c * (x + 0.044715 * (x * x * x))))


def _rms(x, g):
    return x * lax.rsqrt(jnp.mean(x * x, axis=-1, keepdims=True) + EPS) * g


def _bdot(a, b):
    return jnp.dot(a, b, preferred_element_type=F32)


def _split3(f):
    hi = f.astype(BF16).astype(F32)
    r = f - hi
    mid = r.astype(BF16).astype(F32)
    lo = (r - mid).astype(BF16).astype(F32)
    return hi, mid, lo


def _cumsum_rows(f):
    n = f.shape[0]
    k = 1
    while k < n:
        f = f + jnp.concatenate([jnp.zeros((k, f.shape[1]), f.dtype), f[:n - k]], axis=0)
        k *= 2
    return f


def _inproj_kernel(x_ref, g_ref, wq_ref, bq_ref, wk_ref, bk_ref, wvt_ref, bvt_ref, wf_ref, bf_ref, wu_ref, bu_ref,
                   wc_ref, bc_ref, wg_ref, bg_ref,
                   qa_ref, ka_ref, vt_ref, ua_ref, ub_ref, hc_ref, gates_ref, carry_ref, *, tiles_per_seq):
    tm = x_ref.shape[0]
    dh = ATT_HEAD_DIM

    @pl.when(pl.program_id(0) % tiles_per_seq == 0)
    def _():
        carry_ref[...] = jnp.zeros_like(carry_ref)

    h = _rms(x_ref[...], g_ref[...]).astype(BF16)

    fz = _bdot(h, wf_ref[...]) + bf_ref[...]
    logf = jnp.minimum(fz, 0.0) - jnp.log1p(jnp.exp(-jnp.abs(fz)))
    fcum = _cumsum_rows(logf) + carry_ref[...]
    carry_ref[...] = fcum[tm - 1:tm, :]

    lane = lax.broadcasted_iota(jnp.int32, (tm, HEAD_PAD), 1) - dh
    ones = jnp.where((lane >= 0) & (lane < 6), 1.0, 0.0)
    q = _bdot(h, wq_ref[...]) + bq_ref[...]
    k = _bdot(h, wk_ref[...]) + bk_ref[...]
    for hd in range(ATT_HEADS):
        blk = slice((hd // 2) * HEAD_PAD, (hd // 2 + 1) * HEAD_PAD)
        qh, kh = q[:, blk], k[:, blk]
        if hd % 2 == 1:
            qh = pltpu.roll(qh, dh, 1)
            kh = pltpu.roll(kh, dh, 1)
        hi, mid, lo = _split3(jnp.broadcast_to(fcum[:, hd:hd + 1], (tm, HEAD_PAD)))
        qx = jnp.where(lane == 0, hi, jnp.where(lane == 1, mid, jnp.where(lane == 2, lo, ones)))
        kx = jnp.where(lane == 3, -hi, jnp.where(lane == 4, -mid, jnp.where(lane == 5, -lo, ones)))
        qa_ref[0, hd] = jnp.where(lane < 0, qh, qx).astype(BF16)
        ka_ref[0, hd] = jnp.where(lane < 0, kh, kx).astype(BF16)

    vt = lax.dot_general(wvt_ref[...], h, (((1,), (1,)), ((), ())), preferred_element_type=F32) + bvt_ref[...]
    row = lax.broadcasted_iota(jnp.int32, (HEAD_PAD - dh, tm), 0)
    tail = jnp.where(row == 0, 1.0, 0.0).astype(BF16)
    for hd in range(ATT_HEADS):
        vt_ref[0, hd, 0:dh, :] = vt[hd * dh:(hd + 1) * dh, :].astype(BF16)
        vt_ref[0, hd, dh:HEAD_PAD, :] = tail

    u = _bdot(h, wu_ref[...]) + bu_ref[...]
    ua_ref[...] = u[:, :LANES]
    ub_ref[...] = u[:, LANES:]
    cz = _bdot(h, wc_ref[...]) + bc_ref[...]
    cw = cz.shape[-1] // 2
    hc_ref[...] = (cz[:, :cw] * _sigmoid(cz[:, cw:])).astype(BF16)
    d = x_ref.shape[-1]
    for j in range(gates_ref.shape[-1] // d):
        gz = _bdot(h, wg_ref[:, j * d:(j + 1) * d]) + bg_ref[:, j * d:(j + 1) * d]
        gates_ref[:, j * d:(j + 1) * d] = _sigmoid(gz).astype(BF16)


def _inproj(x2, b, weights):
    n, d = x2.shape
    l = n // b
    tm = min(TOKEN_TILE, l)
    tps = l // tm
    wc, wg = weights[11], weights[13]
    row = lambda w: pl.BlockSpec((tm, w), lambda i: (i, 0))
    head = jax.ShapeDtypeStruct((b, ATT_HEADS, l, HEAD_PAD), BF16)
    head_t = jax.ShapeDtypeStruct((b, ATT_HEADS, HEAD_PAD, l), BF16)
    hspec = pl.BlockSpec((1, ATT_HEADS, tm, HEAD_PAD), lambda i: (i // tps, 0, i % tps, 0))
    hspec_t = pl.BlockSpec((1, ATT_HEADS, HEAD_PAD, tm), lambda i: (i // tps, 0, 0, i % tps))
    flat = [(LANES, F32), (LANES, F32), (wc.shape[1] // 2, BF16), (wg.shape[1], BF16)]
    return pl.pallas_call(
        functools.partial(_inproj_kernel, tiles_per_seq=tps),
        grid=(n // tm,),
        in_specs=[row(d)] + [_resident(a.shape) for a in weights],
        out_specs=[hspec, hspec, hspec_t] + [row(w) for w, _ in flat],
        out_shape=[head, head, head_t] + [jax.ShapeDtypeStruct((n, w), dt) for w, dt in flat],
        scratch_shapes=[pltpu.VMEM((1, LANES), F32)],
        compiler_params=pltpu.CompilerParams(dimension_semantics=("arbitrary",),
                                             vmem_limit_bytes=VMEM_LIMIT),
        name="inproj",
    )(x2, *weights)


def _fox_kernel(qi_ref, ki_ref, qa_ref, ka_ref, vt_ref, o_ref, m_ref, acc_ref, s_ref):
    p_id = pl.program_id(1)
    qi = qi_ref[p_id]
    ki = ki_ref[p_id]
    t = qa_ref.shape[2]
    dh = ATT_HEAD_DIM

    @pl.when(ki == 0)
    def _():
        m_ref[...] = jnp.full_like(m_ref, NEG_BIG)
        acc_ref[...] = jnp.zeros_like(acc_ref)

    def scores(h, masked):
        st = lax.dot_general(ka_ref[0, h], qa_ref[0, h], (((1,), (1,)), ((), ())),
                             preferred_element_type=F32)
        if masked:
            kpos = lax.broadcasted_iota(jnp.int32, (t, t), 0)
            qpos = lax.broadcasted_iota(jnp.int32, (t, t), 1)
            st = jnp.where(kpos <= qpos, st, NEG_BIG)
        s_ref[h % 2] = st

    def step(masked):
        scores(0, masked)
        for h in range(ATT_HEADS):
            if h + 1 < ATT_HEADS:
                scores(h + 1, masked)
            st = s_ref[h % 2]
            m_prev = m_ref[h]
            m_new = jnp.maximum(m_prev, jnp.max(st, axis=0, keepdims=True))
            p = jnp.exp(st - m_new).astype(BF16)
            acc_ref[h] = jnp.exp(m_prev - m_new) * acc_ref[h] + _bdot(vt_ref[0, h], p)
            m_ref[h] = m_new

    @pl.when(ki < qi)
    def _():
        step(False)

    @pl.when(ki == qi)
    def _():
        step(True)
        for h in range(ATT_HEADS):
            a = acc_ref[h]
            o = (a / a[dh:dh + 1, :]).T
            o_ref[0, :, h * dh:(h + 1) * dh] = o[:, :dh].astype(BF16)


def _fox_attention(qa, ka, vt):
    b, nh, l, _ = qa.shape
    t = min(ATT_TILE, l)
    nt = l // t
    pairs = [(q, k) for q in range(nt) for k in range(q + 1)]
    qi = jnp.asarray([p[0] for p in pairs], jnp.int32)
    ki = jnp.asarray([p[1] for p in pairs], jnp.int32)
    qspec = pl.BlockSpec((1, nh, t, HEAD_PAD), lambda bi, p, qi, ki: (bi, 0, qi[p], 0))
    kspec = pl.BlockSpec((1, nh, t, HEAD_PAD), lambda bi, p, qi, ki: (bi, 0, ki[p], 0))
    vspec = pl.BlockSpec((1, nh, HEAD_PAD, t), lambda bi, p, qi, ki: (bi, 0, 0, ki[p]))
    return pl.pallas_call(
        _fox_kernel,
        grid_spec=pltpu.PrefetchScalarGridSpec(
            num_scalar_prefetch=2,
            grid=(b, len(pairs)),
            in_specs=[qspec, kspec, vspec],
            out_specs=pl.BlockSpec((1, t, nh * ATT_HEAD_DIM), lambda bi, p, qi, ki: (bi, qi[p], 0)),
            scratch_shapes=[pltpu.VMEM((nh, 1, t), F32), pltpu.VMEM((nh, HEAD_PAD, t), F32),
                            pltpu.VMEM((2, t, t), F32)],
        ),
        out_shape=jax.ShapeDtypeStruct((b, l, nh * ATT_HEAD_DIM), BF16),
        compiler_params=pltpu.CompilerParams(dimension_semantics=("parallel", "arbitrary"),
                                             vmem_limit_bytes=VMEM_LIMIT),
        name="fox_attention",
    )(qi, ki, qa, ka, vt)


def _s5_kernel(ua_ref, ub_ref, m_ref, bc_ref, cc_ref, ar_ref, ai_ref, ya_ref, yb_ref, v_ref, xp_ref):
    rows = ua_ref.shape[0] // SSM_CHUNK
    u = jnp.concatenate([r[pl.ds(s, rows, stride=SSM_CHUNK), :].astype(BF16)
                         for s in range(SSM_CHUNK) for r in (ua_ref, ub_ref)], axis=-1)
    half = ar_ref.shape[-1]
    v_ref[...] = _bdot(u, bc_ref[...])
    ar = ar_ref[...]
    ai = ai_ref[...]

    def group(i, carry):
        xr, xi = carry
        base = pl.multiple_of(i * 8, 8)
        blk = v_ref[pl.ds(base, 8), :]
        prev = []
        for j in range(8):
            prev.append(jnp.concatenate([xr, xi], axis=-1))
            vr = blk[j:j + 1, :half]
            vi = blk[j:j + 1, half:]
            xr, xi = ar * xr - ai * xi + vr, ar * xi + ai * xr + vi
        xp_ref[pl.ds(base, 8), :] = jnp.concatenate(prev, axis=0)
        return xr, xi

    zero = jnp.zeros((1, half), F32)
    lax.fori_loop(0, rows // 8, group, (zero, zero))
    y = _bdot(u, m_ref[...]) + _bdot(xp_ref[...].astype(BF16), cc_ref[...])
    for s in range(SSM_CHUNK):
        ya_ref[pl.ds(s, rows, stride=SSM_CHUNK), :] = y[:, 2 * s * LANES:(2 * s + 1) * LANES]
        yb_ref[pl.ds(s, rows, stride=SSM_CHUNK), :] = y[:, (2 * s + 1) * LANES:(2 * s + 2) * LANES]


def _s5(ua, ub, m, bc, cc, ar, ai, b):
    n = ua.shape[0]
    l = n // b
    rows = l // SSM_CHUNK
    half = pl.BlockSpec((l, LANES), lambda i: (i, 0))
    out = jax.ShapeDtypeStruct((n, LANES), F32)
    return pl.pallas_call(
        _s5_kernel,
        grid=(b,),
        in_specs=[half, half] + [_resident(a.shape) for a in (m, bc, cc, ar, ai)],
        out_specs=[half, half],
        out_shape=[out, out],
        scratch_shapes=[pltpu.VMEM((rows, bc.shape[1]), F32), pltpu.VMEM((rows, bc.shape[1]), F32)],
        compiler_params=pltpu.CompilerParams(dimension_semantics=("parallel",),
                                             vmem_limit_bytes=VMEM_LIMIT),
        name="s5",
    )(ua, ub, m, bc, cc, ar, ai)


def _s5_operators(a_re, a_im, log_dt, b_re, b_im, c_re, c_im, d_skip):
    t = SSM_CHUNK
    g, p = a_re.shape
    hc = d_skip.shape[-1]
    lam = lax.complex(a_re.astype(F32), a_im.astype(F32))
    dt = jnp.exp(log_dt.astype(F32))[:, None]
    a_bar = jnp.exp(lam * dt)
    b_bar = ((a_bar - 1.0) / lam)[..., None] * lax.complex(b_re.astype(F32), b_im.astype(F32))
    c = lax.complex(c_re.astype(F32), c_im.astype(F32))
    tau = jnp.arange(t + 1, dtype=F32)
    apow = jnp.exp((lam * dt)[None] * tau[:, None, None])
    eye_g = jnp.eye(g, dtype=F32)
    kern = jnp.real(jnp.einsum('gop,tgp,gpi->tgio', c, apow[:t], b_bar))
    kern = kern.at[0].add(jnp.eye(hc, dtype=F32)[None] * d_skip.astype(F32)[:, :, None])
    kb = jnp.einsum('tgio,gk->tgiko', kern, eye_g).reshape(t, g * hc, g * hc).astype(BF16)
    zero = jnp.zeros((g * hc, g * hc), BF16)
    m = jnp.concatenate([jnp.concatenate([kb[tt - ss] if tt >= ss else zero for tt in range(t)], axis=1)
                         for ss in range(t)], axis=0)
    bs = apow[:t][::-1][:, :, :, None] * b_bar[None]
    bcx = jnp.einsum('sgph,gk->sghkp', bs, eye_g.astype(bs.dtype)).reshape(t * g * hc, g * p)
    bc = jnp.concatenate([jnp.real(bcx), jnp.imag(bcx)], axis=1)
    ct = c[None] * apow[1:][:, :, None, :]
    ccx = jnp.einsum('tgop,gk->gptko', ct, eye_g.astype(ct.dtype)).reshape(g * p, t * g * hc)
    cc = jnp.concatenate([jnp.real(ccx), -jnp.imag(ccx)], axis=0)
    a_t = apow[t].reshape(1, g * p)
    return (m, bc.astype(BF16), cc.astype(BF16),
            jnp.real(a_t).astype(F32), jnp.imag(a_t).astype(F32))


def _conv_kernel(h_ref, w_ref, b_ref, g_ref, beta_ref, o_ref, sh_ref):
    t = h_ref.shape[0]
    cw = h_ref.shape[1]

    @pl.when(pl.program_id(1) == 0)
    def _():
        sh_ref[:, 0:CONV_HALO + 16, :] = jnp.zeros((8, CONV_HALO + 16, cw), F32)

    @pl.when(pl.program_id(1) > 0)
    def _():
        sh_ref[:, 0:CONV_HALO + 16, :] = sh_ref[:, t:t + CONV_HALO + 16, :]

    hv = h_ref[...].astype(F32)
    for r in range(8):
        sh_ref[r, CONV_HALO + 8 + r:CONV_HALO + 8 + r + t, :] = hv
    rb = min(CONV_ROWS, t)
    for r0 in range(0, t, rb):
        acc = jnp.zeros((rb, cw), F32)
        for j in range(CONV_K):
            delay = CONV_K - 1 - j
            base = CONV_HALO + 8 + r0 - 8 * (delay // 8)
            acc = acc + w_ref[j:j + 1, :] * sh_ref[delay % 8, base:base + rb, :]
        y = acc + b_ref[...]
        mu = jnp.mean(y, axis=-1, keepdims=True)
        yc = y - mu
        var = jnp.mean(yc * yc, axis=-1, keepdims=True)
        z = yc * lax.rsqrt(var + EPS) * g_ref[...] + beta_ref[...]
        o_ref[r0:r0 + rb, :] = _silu(z).astype(BF16)


def _conv(hc, w, bias, g, beta, b, l):
    t = min(CONV_TILE, l)
    nt = l // t
    cw = hc.shape[1]
    return pl.pallas_call(
        _conv_kernel,
        grid=(b, nt),
        in_specs=[pl.BlockSpec((t, cw), lambda bi, i: (bi * nt + i, 0))]
                 + [_resident(a.shape) for a in (w, bias, g, beta)],
        out_specs=pl.BlockSpec((t, cw), lambda bi, i: (bi * nt + i, 0)),
        out_shape=jax.ShapeDtypeStruct(hc.shape, BF16),
        scratch_shapes=[pltpu.VMEM((8, t + CONV_HALO + 16, cw), F32)],
        compiler_params=pltpu.CompilerParams(dimension_semantics=("parallel", "arbitrary"),
                                             vmem_limit_bytes=VMEM_LIMIT),
        name="conv",
    )(hc, w, bias, g, beta)


def _pack_halves(h):
    w = h.shape[-1] // 2
    bits = lax.bitcast_convert_type(h.astype(BF16).astype(F32), jnp.uint32)
    return bits[:, :w] | (bits[:, w:] >> 16)


def _unpack_halves(u):
    hi = lax.bitcast_convert_type(u & jnp.uint32(0xFFFF0000), F32)
    lo = lax.bitcast_convert_type(u << 16, F32)
    return jnp.concatenate([hi, lo], axis=-1).astype(BF16)


def _merge_kernel(x_ref, att_ref, ya_ref, yb_ref, conv_ref, gates_ref, wglu_ref, wa_ref, ws_ref, wc_ref, wo_ref,
                  gf_ref, xo_ref, h_ref, *, packed):
    d = x_ref.shape[-1]
    y = jnp.concatenate([ya_ref[...], yb_ref[...]], axis=-1)
    sw = y.shape[-1]
    yg = _bdot(_gelu_tanh(y).astype(BF16), wglu_ref[...])
    ssm = (yg[:, :sw] * _sigmoid(yg[:, sw:])).astype(BF16)
    merged = gates_ref[:, 0:d].astype(F32) * _bdot(att_ref[...], wa_ref[...])
    merged = merged + gates_ref[:, d:2 * d].astype(F32) * _bdot(ssm, ws_ref[...])
    merged = merged + gates_ref[:, 2 * d:3 * d].astype(F32) * _bdot(conv_ref[...], wc_ref[...])
    xn = x_ref[...] + _bdot(merged.astype(BF16), wo_ref[...])
    xo_ref[...] = xn
    hn = _rms(xn, gf_ref[...])
    h_ref[...] = _pack_halves(hn) if packed else hn.astype(BF16)


def _merge(x2, att, ya, yb, conv, gates, wglu, wa, ws, wc, wo, gf, packed):
    n, d = x2.shape
    tm = min(TOKEN_TILE, n)
    row = lambda w: pl.BlockSpec((tm, w), lambda i: (i, 0))
    hw, hdt = (d // 2, jnp.uint32) if packed else (d, BF16)
    return pl.pallas_call(
        functools.partial(_merge_kernel, packed=packed),
        grid=(n // tm,),
        in_specs=[row(d), row(att.shape[1]), row(LANES), row(LANES), row(conv.shape[1]), row(gates.shape[1])]
                 + [_resident(a.shape) for a in (wglu, wa, ws, wc, wo, gf)],
        out_specs=[row(d), row(hw)],
        out_shape=[jax.ShapeDtypeStruct((n, d), F32), jax.ShapeDtypeStruct((n, hw), hdt)],
        compiler_params=pltpu.CompilerParams(dimension_semantics=("parallel",),
                                             vmem_limit_bytes=VMEM_LIMIT),
        name="merge",
    )(x2, att, ya, yb, conv, gates, wglu, wa, ws, wc, wo, gf)


def _ffn_kernel(x_ref, h_ref, wg_ref, wu_ref, wd_ref, xo_ref, *, chunks):
    h = h_ref[...]
    f = wg_ref.shape[1]
    fc = f // chunks
    acc = x_ref[...]
    for j in range(chunks):
        sl = slice(j * fc, (j + 1) * fc)
        act = (_silu(_bdot(h, wg_ref[:, sl])) * _bdot(h, wu_ref[:, sl])).astype(BF16)
        acc = acc + _bdot(act, wd_ref[sl, :])
    xo_ref[...] = acc


def _ffn(x2, h, wg, wu, wd):
    n, d = x2.shape
    tm = min(TOKEN_TILE, n)
    row = pl.BlockSpec((tm, d), lambda i: (i, 0))
    f = wg.shape[1]
    chunks = 2 if f % (2 * LANES) == 0 else 1
    return pl.pallas_call(
        functools.partial(_ffn_kernel, chunks=chunks),
        grid=(n // tm,),
        in_specs=[row, row] + [_resident(a.shape) for a in (wg, wu, wd)],
        out_specs=row,
        out_shape=jax.ShapeDtypeStruct((n, d), F32),
        compiler_params=pltpu.CompilerParams(dimension_semantics=("parallel",),
                                             vmem_limit_bytes=VMEM_LIMIT),
        name="ffn",
    )(x2, h, wg, wu, wd)


META_E1, META_E2, META_P1, META_P2, META_R1, META_R2 = range(6)


def _split2(a):
    hi = a.astype(BF16)
    return hi, (a - hi.astype(F32)).astype(BF16)


def _router_kernel(x_ref, g_ref, wr_ref, br_ref, meta_ref, counts_ref):
    @pl.when(pl.program_id(0) == 0)
    def _():
        counts_ref[...] = jnp.zeros_like(counts_ref)

    h_hi, h_lo = _split2(_rms(x_ref[...], g_ref[...]))
    w_hi, w_lo = _split2(wr_ref[...])
    logits = _bdot(h_hi, w_hi) + (_bdot(h_hi, w_lo) + _bdot(h_lo, w_hi)) + br_ref[...]
    tm = logits.shape[0]
    lane = lax.broadcasted_iota(jnp.int32, logits.shape, 1)
    logits = jnp.where(lane < N_EXPERTS, logits, NEG_BIG)
    m1 = jnp.max(logits, axis=-1, keepdims=True)
    i1 = jnp.min(jnp.where(logits == m1, lane, LANES), axis=-1, keepdims=True)
    rest = jnp.where(lane == i1, NEG_BIG, logits)
    m2 = jnp.max(rest, axis=-1, keepdims=True)
    i2 = jnp.min(jnp.where(rest == m2, lane, LANES), axis=-1, keepdims=True)
    e = jnp.exp(m2 - m1)
    p1 = 1.0 / (1.0 + e)
    p2 = e / (1.0 + e)
    assigned = jnp.where((lane == i1) | (lane == i2), 1.0, 0.0)
    r = lax.broadcasted_iota(jnp.int32, (tm, tm), 0)
    c = lax.broadcasted_iota(jnp.int32, (tm, tm), 1)
    before = _bdot((c < r).astype(BF16), assigned.astype(BF16)) + counts_ref[...]
    rank1 = jnp.sum(jnp.where(lane == i1, before, 0.0), axis=-1, keepdims=True)
    rank2 = jnp.sum(jnp.where(lane == i2, before, 0.0), axis=-1, keepdims=True)
    meta = jnp.where(lane == META_E1, i1.astype(F32), 0.0)
    for idx, val in ((META_E2, i2.astype(F32)), (META_P1, p1), (META_P2, p2), (META_R1, rank1), (META_R2, rank2)):
        meta = jnp.where(lane == idx, val, meta)
    meta_ref[...] = meta
    counts_ref[...] += jnp.sum(assigned, axis=0, keepdims=True)


def _router(x2, g, wr, br):
    n, d = x2.shape
    tm = min(TOKEN_TILE, n)
    return pl.pallas_call(
        _router_kernel,
        grid=(n // tm,),
        in_specs=[pl.BlockSpec((tm, d), lambda i: (i, 0))] + [_resident(a.shape) for a in (g, wr, br)],
        out_specs=[pl.BlockSpec((tm, LANES), lambda i: (i, 0)), pl.BlockSpec((1, LANES), lambda i: (0, 0))],
        out_shape=[jax.ShapeDtypeStruct((n, LANES), F32), jax.ShapeDtypeStruct((1, LANES), F32)],
        compiler_params=pltpu.CompilerParams(dimension_semantics=("arbitrary",),
                                             vmem_limit_bytes=VMEM_LIMIT),
        name="router",
    )(x2, g, wr, br)


def _dispatch_plan(meta, counts, tm, n_tiles):
    cnt = counts[0, :N_EXPERTS].astype(jnp.int32)
    tiles = (cnt + tm - 1) // tm
    tile_end = jnp.cumsum(tiles)
    row_start = (tile_end - tiles) * tm
    t_idx = jnp.arange(n_tiles, dtype=jnp.int32)
    n_used = tile_end[-1]
    tile_expert = jnp.sum((t_idx[:, None] >= tile_end[None, :]).astype(jnp.int32), axis=1)
    last_expert = jnp.sum((n_used - 1 >= tile_end).astype(jnp.int32))
    tile_expert = jnp.where(t_idx < n_used, tile_expert, last_expert)
    experts = jnp.arange(N_EXPERTS, dtype=jnp.int32)

    def pos(e_lane, r_lane):
        e = meta[:, e_lane].astype(jnp.int32)
        start = jnp.sum(jnp.where(e[:, None] == experts[None, :], row_start[None, :], 0), axis=1)
        return start + meta[:, r_lane].astype(jnp.int32)

    pos2 = jnp.stack([pos(META_E1, META_R1), pos(META_E2, META_R2)], axis=1)
    return pos2, tile_expert, n_used.reshape(1)


def _dispatch_kernel(pos_ref, h_ref, init_ref, sorted_ref, sem):
    del init_ref
    tm = h_ref.shape[0]

    def row_copy(r, k):
        return pltpu.make_async_copy(h_ref.at[pl.ds(r, 1), :],
                                     sorted_ref.at[pl.ds(pos_ref[0, 0, 2 * r + k], 1), :], sem)

    def start(r, carry):
        row_copy(r, 0).start()
        row_copy(r, 1).start()
        return carry

    lax.fori_loop(0, tm, start, 0, unroll=DMA_UNROLL)
    tile_copy = pltpu.make_async_copy(h_ref, sorted_ref.at[pl.ds(0, tm), :], sem)
    tile_copy.wait()
    tile_copy.wait()


def _dispatch(hp, pos2, n_rows):
    n, w = hp.shape
    tm = min(MOE_TILE, n)
    pos_tiles = pos2.reshape(n // tm, 1, 2 * tm)
    return pl.pallas_call(
        _dispatch_kernel,
        grid=(n // tm,),
        in_specs=[pl.BlockSpec((1, 1, 2 * tm), lambda i: (i, 0, 0), memory_space=pltpu.SMEM),
                  pl.BlockSpec((tm, w), lambda i: (i, 0)),
                  pl.BlockSpec(memory_space=pl.ANY)],
        out_specs=pl.BlockSpec(memory_space=pl.ANY),
        out_shape=jax.ShapeDtypeStruct((n_rows, w), hp.dtype),
        scratch_shapes=[pltpu.SemaphoreType.DMA(())],
        input_output_aliases={2: 0},
        compiler_params=pltpu.CompilerParams(dimension_semantics=("arbitrary",),
                                             vmem_limit_bytes=VMEM_LIMIT),
        name="dispatch",
    )(pos_tiles, hp, jnp.zeros((n_rows, w), hp.dtype))


def _gffn_kernel(te_ref, nu_ref, rows_ref, wg_ref, wu_ref, wd_ref, y_ref, *, chunks):
    del te_ref

    @pl.when(pl.program_id(0) < nu_ref[0])
    def _():
        h = _unpack_halves(rows_ref[...])
        fc = wg_ref.shape[3] // chunks
        acc = None
        for j in range(chunks):
            sl = slice(j * fc, (j + 1) * fc)
            act = (_silu(_bdot(h, wg_ref[0, 0, :, sl])) * _bdot(h, wu_ref[0, 0, :, sl])).astype(BF16)
            part = _bdot(act, wd_ref[0, 0, sl, :])
            acc = part if acc is None else acc + part
        y_ref[...] = acc

    @pl.when(pl.program_id(0) >= nu_ref[0])
    def _():
        y_ref[...] = jnp.zeros_like(y_ref)


def _gffn(rows, tile_expert, n_used, wg, wu, wd, layer):
    n_rows, w = rows.shape
    _, ne, d, f = wg.shape
    tm = min(MOE_TILE, n_rows)
    chunks = 2 if f % (2 * LANES) == 0 else 1
    wspec = lambda shape: pl.BlockSpec(shape, lambda t, te, nu: (layer, te[t], 0, 0),
                                       pipeline_mode=pl.Buffered(1))
    return pl.pallas_call(
        functools.partial(_gffn_kernel, chunks=chunks),
        grid_spec=pltpu.PrefetchScalarGridSpec(
            num_scalar_prefetch=2,
            grid=(n_rows // tm,),
            in_specs=[pl.BlockSpec((tm, w), lambda t, te, nu: (t, 0)),
                      wspec((1, 1, d, f)), wspec((1, 1, d, f)), wspec((1, 1, f, d))],
            out_specs=pl.BlockSpec((tm, d), lambda t, te, nu: (t, 0)),
        ),
        out_shape=jax.ShapeDtypeStruct((n_rows, d), F32),
        compiler_params=pltpu.CompilerParams(dimension_semantics=("arbitrary",),
                                             vmem_limit_bytes=VMEM_LIMIT),
        name="grouped_ffn",
    )(tile_expert, n_used, rows, wg, wu, wd)


def _combine_kernel(pos_ref, x_ref, meta_ref, g_ref, y_ref, o_ref, buf1_ref, buf2_ref, sem, *, final):
    tm = x_ref.shape[0]
    bufs = (buf1_ref, buf2_ref)

    def row_copy(r, k):
        return pltpu.make_async_copy(y_ref.at[pl.ds(pos_ref[0, 0, 2 * r + k], 1), :],
                                     bufs[k].at[pl.ds(r, 1), :], sem)

    def start(r, carry):
        row_copy(r, 0).start()
        row_copy(r, 1).start()
        return carry

    lax.fori_loop(0, tm, start, 0, unroll=DMA_UNROLL)
    for buf in bufs:
        pltpu.make_async_copy(y_ref.at[pl.ds(0, tm), :], buf, sem).wait()
    meta = meta_ref[...]
    p1 = meta[:, META_P1:META_P1 + 1]
    p2 = meta[:, META_P2:META_P2 + 1]
    xn = x_ref[...] + p1 * buf1_ref[...] + p2 * buf2_ref[...]
    o_ref[...] = _rms(xn, g_ref[...]) if final else xn


def _combine(x2, meta, pos2, y, g, final):
    n, d = x2.shape
    tm = min(MOE_TILE, n)
    pos_tiles = pos2.reshape(n // tm, 1, 2 * tm)
    return pl.pallas_call(
        functools.partial(_combine_kernel, final=final),
        grid=(n // tm,),
        in_specs=[pl.BlockSpec((1, 1, 2 * tm), lambda i: (i, 0, 0), memory_space=pltpu.SMEM),
                  pl.BlockSpec((tm, d), lambda i: (i, 0)),
                  pl.BlockSpec((tm, LANES), lambda i: (i, 0)),
                  _resident(g.shape),
                  pl.BlockSpec(memory_space=pl.ANY)],
        out_specs=pl.BlockSpec((tm, d), lambda i: (i, 0)),
        out_shape=jax.ShapeDtypeStruct((n, d), F32),
        scratch_shapes=[pltpu.VMEM((tm, d), F32), pltpu.VMEM((tm, d), F32), pltpu.SemaphoreType.DMA(())],
        compiler_params=pltpu.CompilerParams(dimension_semantics=("arbitrary",),
                                             vmem_limit_bytes=VMEM_LIMIT),
        name="combine",
    )(pos_tiles, x2, meta, g, y)


def _moe(x2, hp, g_ffn_row, wr, br, wg, wu, wd, layer, g_out, final):
    n, _ = x2.shape
    tm = min(MOE_TILE, n)
    n_tiles = 2 * n // tm + N_EXPERTS
    meta, counts = _router(x2, g_ffn_row, wr, br)
    pos2, tile_expert, n_used = _dispatch_plan(meta, counts, tm, n_tiles)
    rows = _dispatch(hp, pos2, n_tiles * tm)
    y = _gffn(rows, tile_expert, n_used, wg, wu, wd, layer)
    return _combine(x2, meta, pos2, y, g_out, final)


def _final_kernel(x_ref, g_ref, o_ref):
    o_ref[...] = _rms(x_ref[...], g_ref[...])


def _final_norm(x2, g):
    n, d = x2.shape
    tm = min(TOKEN_TILE, n)
    row = pl.BlockSpec((tm, d), lambda i: (i, 0))
    return pl.pallas_call(
        _final_kernel,
        grid=(n // tm,),
        in_specs=[row, _resident(g.shape)],
        out_specs=row,
        out_shape=jax.ShapeDtypeStruct((n, d), F32),
        compiler_params=pltpu.CompilerParams(dimension_semantics=("parallel",),
                                             vmem_limit_bytes=VMEM_LIMIT),
        name="final_norm",
    )(x2, g)


def _pad_cols(a, width):
    return jnp.pad(a, ((0, 0), (0, width - a.shape[1])))


def kernel(x, g_mix, w_in, b_in, ssm_a_re, ssm_a_im, ssm_log_dt, ssm_b_re, ssm_b_im, ssm_c_re, ssm_c_im, ssm_d, ssm_w_glu, conv_w, conv_b, conv_ln_g, conv_ln_b, w_att_out, w_ssm_out, w_conv_out, w_o, g_ffn, ffn_w_gate, ffn_w_up, ffn_w_down, moe_w_router, moe_b_router, moe_w_gate, moe_w_up, moe_w_down, g_final):
    b, l, d = x.shape
    depth = w_in.shape[0]
    n = b * l
    aw = ATT_HEADS * ATT_HEAD_DIM
    sw = SSM_GROUPS * SSM_GROUP
    cw = conv_w.shape[-1]
    q_end, k_end, v_end = aw, 2 * aw, 3 * aw
    f_end = v_end + ATT_HEADS
    u_end = f_end + sw
    c_end = u_end + 2 * cw
    scale = ATT_HEAD_DIM ** -0.5
    row2 = lambda v: v.reshape(1, -1).astype(F32)

    moe_wg, moe_wu, moe_wd = (a.astype(BF16) for a in (moe_w_gate, moe_w_up, moe_w_down))
    x2 = x.reshape(n, d)
    for layer in range(depth):
        w = w_in[layer]
        bias = b_in[layer]
        pad = lambda v: _pad_cols(v, LANES)
        col = lambda v: v.reshape(-1, 1).astype(F32)
        weights = (row2(g_mix[layer]),
                   (w[:, :q_end] * scale).astype(BF16), row2(bias[:q_end] * scale),
                   w[:, q_end:k_end].astype(BF16), row2(bias[q_end:k_end]),
                   w[:, k_end:v_end].T.astype(BF16), col(bias[k_end:v_end]),
                   pad(w[:, v_end:f_end]).astype(BF16), pad(row2(bias[v_end:f_end])),
                   w[:, f_end:u_end].astype(BF16), row2(bias[f_end:u_end]),
                   w[:, u_end:c_end].astype(BF16), row2(bias[u_end:c_end]),
                   w[:, c_end:].astype(BF16), row2(bias[c_end:]))
        qa, ka, vt, ua, ub, hc, gates = _inproj(x2, b, weights)
        att = _fox_attention(qa, ka, vt).reshape(n, aw)

        m, bc, cc, ar, ai = _s5_operators(ssm_a_re[layer], ssm_a_im[layer], ssm_log_dt[layer],
                                          ssm_b_re[layer], ssm_b_im[layer], ssm_c_re[layer],
                                          ssm_c_im[layer], ssm_d[layer])
        ya, yb = _s5(ua, ub, m, bc, cc, ar, ai, b)

        wconv = jnp.pad(conv_w[layer].astype(F32), ((0, CONV_HALO - CONV_K), (0, 0)))
        conv = _conv(hc, wconv, row2(conv_b[layer]), row2(conv_ln_g[layer]), row2(conv_ln_b[layer]), b, l)

        is_moe = layer % 2 == 1
        last = layer == depth - 1
        x2, h = _merge(x2, att, ya, yb, conv, gates, ssm_w_glu[layer].astype(BF16),
                       w_att_out[layer].astype(BF16), w_ssm_out[layer].astype(BF16),
                       w_conv_out[layer].astype(BF16), w_o[layer].astype(BF16), row2(g_ffn[layer]), is_moe)

        i = layer // 2
        if is_moe:
            x2 = _moe(x2, h, row2(g_ffn[layer]), _pad_cols(moe_w_router[i].astype(F32), LANES),
                      _pad_cols(row2(moe_b_router[i]), LANES), moe_wg, moe_wu, moe_wd, i,
                      row2(g_final), last)
        else:
            x2 = _ffn(x2, h, ffn_w_gate[i].astype(BF16), ffn_w_up[i].astype(BF16), ffn_w_down[i].astype(BF16))
            if last:
                x2 = _final_norm(x2, row2(g_final))
    return x2.reshape(b, l, d)
```

```python
import functools
import math

import jax
import jax.numpy as jnp
from jax import lax
from jax.experimental import pallas as pl
from jax.experimental.pallas import tpu as pltpu

F32 = jnp.float32
BF16 = jnp.bfloat16
HIGHEST = lax.Precision.HIGHEST

EPS = 1e-6
ATT_HEADS = 8
ATT_HEAD_DIM = 64
SSM_GROUPS = 16
SSM_GROUP = 16
SSM_STATE = 64
CONV_K = 31
N_EXPERTS = 8
LANES = 128
HEAD_PAD = 128
SSM_CHUNK = 8
CONV_HALO = 32
NEG_BIG = -1e30

TOKEN_TILE = 512
ATT_TILE = 512
CONV_ROWS = 64
MOE_TILE = 512
DMA_UNROLL = 8
VMEM_LIMIT = 56 * 1024 * 1024


def _resident(shape):
    nd = len(shape)
    return pl.BlockSpec(shape, lambda *_: (0,) * nd, pipeline_mode=pl.Buffered(1))


def _sigmoid(x):
    return 1.0 / (1.0 + jnp.exp(-x))


def _silu(x):
    return x * _sigmoid(x)


def _gelu_tanh(x):
    c = math.sqrt(2.0 / math.pi)
    return 0.5 * x * (1.0 + jnp.tanh(c * (x + 0.044715 * (x * x * x))))


def _rms(x, g):
    return x * lax.rsqrt(jnp.mean(x * x, axis=-1, keepdims=True) + EPS) * g


def _bdot(a, b):
    return jnp.dot(a, b, preferred_element_type=F32)


def _split3(f):
    hi = f.astype(BF16).astype(F32)
    r = f - hi
    mid = r.astype(BF16).astype(F32)
    lo = (r - mid).astype(BF16).astype(F32)
    return hi, mid, lo


def _cumsum_rows(f):
    n = f.shape[0]
    k = 1
    while k < n:
        f = f + jnp.concatenate([jnp.zeros((k, f.shape[1]), f.dtype), f[:n - k]], axis=0)
        k *= 2
    return f


def _inproj_kernel(x_ref, g_ref, wq_ref, bq_ref, wk_ref, bk_ref, wvt_ref, bvt_ref, wf_ref, bf_ref, wu_ref, bu_ref,
                   wc_ref, bc_ref, wg_ref, bg_ref,
                   qa_ref, ka_ref, vt_ref, ua_ref, ub_ref, hc_ref, gates_ref, carry_ref, *, tiles_per_seq):
    tm = x_ref.shape[0]
    dh = ATT_HEAD_DIM

    @pl.when(pl.program_id(0) % tiles_per_seq == 0)
    def _():
        carry_ref[...] = jnp.zeros_like(carry_ref)

    h = _rms(x_ref[...], g_ref[...]).astype(BF16)

    fz = _bdot(h, wf_ref[...]) + bf_ref[...]
    logf = jnp.minimum(fz, 0.0) - jnp.log1p(jnp.exp(-jnp.abs(fz)))
    fcum = _cumsum_rows(logf) + carry_ref[...]
    carry_ref[...] = fcum[tm - 1:tm, :]

    lane = lax.broadcasted_iota(jnp.int32, (tm, HEAD_PAD), 1) - dh
    ones = jnp.where((lane >= 0) & (lane < 6), 1.0, 0.0)
    q = _bdot(h, wq_ref[...]) + bq_ref[...]
    k = _bdot(h, wk_ref[...]) + bk_ref[...]
    for hd in range(ATT_HEADS):
        blk = slice((hd // 2) * HEAD_PAD, (hd // 2 + 1) * HEAD_PAD)
        qh, kh = q[:, blk], k[:, blk]
        if hd % 2 == 1:
            qh = pltpu.roll(qh, dh, 1)
            kh = pltpu.roll(kh, dh, 1)
        hi, mid, lo = _split3(jnp.broadcast_to(fcum[:, hd:hd + 1], (tm, HEAD_PAD)))
        qx = jnp.where(lane == 0, hi, jnp.where(lane == 1, mid, jnp.where(lane == 2, lo, ones)))
        kx = jnp.where(lane == 3, -hi, jnp.where(lane == 4, -mid, jnp.where(lane == 5, -lo, ones)))
        qa_ref[0, hd] = jnp.where(lane < 0, qh, qx).astype(BF16)
        ka_ref[0, hd] = jnp.where(lane < 0, kh, kx).astype(BF16)

    vt = lax.dot_general(wvt_ref[...], h, (((1,), (1,)), ((), ())), preferred_element_type=F32) + bvt_ref[...]
    row = lax.broadcasted_iota(jnp.int32, (HEAD_PAD - dh, tm), 0)
    tail = jnp.where(row == 0, 1.0, 0.0).astype(BF16)
    for hd in range(ATT_HEADS):
        vt_ref[0, hd, 0:dh, :] = vt[hd * dh:(hd + 1) * dh, :].astype(BF16)
        vt_ref[0, hd, dh:HEAD_PAD, :] = tail

    u = _bdot(h, wu_ref[...]) + bu_ref[...]
    ua_ref[...] = u[:, :LANES]
    ub_ref[...] = u[:, LANES:]
    cz = _bdot(h, wc_ref[...]) + bc_ref[...]
    cw = cz.shape[-1] // 2
    hc_ref[...] = (cz[:, :cw] * _sigmoid(cz[:, cw:])).astype(BF16)
    d = x_ref.shape[-1]
    for j in range(gates_ref.shape[-1] // d):
        gz = _bdot(h, wg_ref[:, j * d:(j + 1) * d]) + bg_ref[:, j * d:(j + 1) * d]
        gates_ref[:, j * d:(j + 1) * d] = _sigmoid(gz).astype(BF16)


def _inproj(x2, b, weights):
    n, d = x2.shape
    l = n // b
    tm = min(TOKEN_TILE, l)
    tps = l // tm
    wc, wg = weights[11], weights[13]
    row = lambda w: pl.BlockSpec((tm, w), lambda i: (i, 0))
    head = jax.ShapeDtypeStruct((b, ATT_HEADS, l, HEAD_PAD), BF16)
    head_t = jax.ShapeDtypeStruct((b, ATT_HEADS, HEAD_PAD, l), BF16)
    hspec = pl.BlockSpec((1, ATT_HEADS, tm, HEAD_PAD), lambda i: (i // tps, 0, i % tps, 0))
    hspec_t = pl.BlockSpec((1, ATT_HEADS, HEAD_PAD, tm), lambda i: (i // tps, 0, 0, i % tps))
    flat = [(LANES, F32), (LANES, F32), (wc.shape[1] // 2, BF16), (wg.shape[1], BF16)]
    return pl.pallas_call(
        functools.partial(_inproj_kernel, tiles_per_seq=tps),
        grid=(n // tm,),
        in_specs=[row(d)] + [_resident(a.shape) for a in weights],
        out_specs=[hspec, hspec, hspec_t] + [row(w) for w, _ in flat],
        out_shape=[head, head, head_t] + [jax.ShapeDtypeStruct((n, w), dt) for w, dt in flat],
        scratch_shapes=[pltpu.VMEM((1, LANES), F32)],
        compiler_params=pltpu.CompilerParams(dimension_semantics=("arbitrary",),
                                             vmem_limit_bytes=VMEM_LIMIT),
        name="inproj",
    )(x2, *weights)


def _fox_kernel(qi_ref, ki_ref, qa_ref, ka_ref, vt_ref, o_ref, m_ref, acc_ref, s_ref):
    p_id = pl.program_id(1)
    qi = qi_ref[p_id]
    ki = ki_ref[p_id]
    t = qa_ref.shape[2]
    dh = ATT_HEAD_DIM

    @pl.when(ki == 0)
    def _():
        m_ref[...] = jnp.full_like(m_ref, NEG_BIG)
        acc_ref[...] = jnp.zeros_like(acc_ref)

    def scores(h, masked):
        st = lax.dot_general(ka_ref[0, h], qa_ref[0, h], (((1,), (1,)), ((), ())),
                             preferred_element_type=F32)
        if masked:
            kpos = lax.broadcasted_iota(jnp.int32, (t, t), 0)
            qpos = lax.broadcasted_iota(jnp.int32, (t, t), 1)
            st = jnp.where(kpos <= qpos, st, NEG_BIG)
        s_ref[h % 2] = st

    def step(masked):
        scores(0, masked)
        for h in range(ATT_HEADS):
            if h + 1 < ATT_HEADS:
                scores(h + 1, masked)
            st = s_ref[h % 2]
            m_prev = m_ref[h]
            m_new = jnp.maximum(m_prev, jnp.max(st, axis=0, keepdims=True))
            p = jnp.exp(st - m_new).astype(BF16)
            acc_ref[h] = jnp.exp(m_prev - m_new) * acc_ref[h] + _bdot(vt_ref[0, h], p)
            m_ref[h] = m_new

    @pl.when(ki < qi)
    def _():
        step(False)

    @pl.when(ki == qi)
    def _():
        step(True)
        for h in range(ATT_HEADS):
            a = acc_ref[h]
            o = (a / a[dh:dh + 1, :]).T
            o_ref[0, :, h * dh:(h + 1) * dh] = o[:, :dh].astype(BF16)


def _fox_attention(qa, ka, vt):
    b, nh, l, _ = qa.shape
    t = min(ATT_TILE, l)
    nt = l // t
    pairs = [(q, k) for q in range(nt) for k in range(q + 1)]
    qi = jnp.asarray([p[0] for p in pairs], jnp.int32)
    ki = jnp.asarray([p[1] for p in pairs], jnp.int32)
    qspec = pl.BlockSpec((1, nh, t, HEAD_PAD), lambda bi, p, qi, ki: (bi, 0, qi[p], 0))
    kspec = pl.BlockSpec((1, nh, t, HEAD_PAD), lambda bi, p, qi, ki: (bi, 0, ki[p], 0))
    vspec = pl.BlockSpec((1, nh, HEAD_PAD, t), lambda bi, p, qi, ki: (bi, 0, 0, ki[p]))
    return pl.pallas_call(
        _fox_kernel,
        grid_spec=pltpu.PrefetchScalarGridSpec(
            num_scalar_prefetch=2,
            grid=(b, len(pairs)),
            in_specs=[qspec, kspec, vspec],
            out_specs=pl.BlockSpec((1, t, nh * ATT_HEAD_DIM), lambda bi, p, qi, ki: (bi, qi[p], 0)),
            scratch_shapes=[pltpu.VMEM((nh, 1, t), F32), pltpu.VMEM((nh, HEAD_PAD, t), F32),
                            pltpu.VMEM((2, t, t), F32)],
        ),
        out_shape=jax.ShapeDtypeStruct((b, l, nh * ATT_HEAD_DIM), BF16),
        compiler_params=pltpu.CompilerParams(dimension_semantics=("parallel", "arbitrary"),
                                             vmem_limit_bytes=VMEM_LIMIT),
        name="fox_attention",
    )(qi, ki, qa, ka, vt)


def _s5_kernel(ua_ref, ub_ref, m_ref, bc_ref, cc_ref, ar_ref, ai_ref, ya_ref, yb_ref, v_ref, xp_ref):
    rows = ua_ref.shape[0] // SSM_CHUNK
    u = jnp.concatenate([r[pl.ds(s, rows, stride=SSM_CHUNK), :].astype(BF16)
                         for s in range(SSM_CHUNK) for r in (ua_ref, ub_ref)], axis=-1)
    half = ar_ref.shape[-1]
    v_ref[...] = _bdot(u, bc_ref[...])
    ar = ar_ref[...]
    ai = ai_ref[...]

    def group(i, carry):
        xr, xi = carry
        base = pl.multiple_of(i * 8, 8)
        blk = v_ref[pl.ds(base, 8), :]
        prev = []
        for j in range(8):
            prev.append(jnp.concatenate([xr, xi], axis=-1))
            vr = blk[j:j + 1, :half]
            vi = blk[j:j + 1, half:]
            xr, xi = ar * xr - ai * xi + vr, ar * xi + ai * xr + vi
        xp_ref[pl.ds(base, 8), :] = jnp.concatenate(prev, axis=0)
        return xr, xi

    zero = jnp.zeros((1, half), F32)
    lax.fori_loop(0, rows // 8, group, (zero, zero))
    y = _bdot(u, m_ref[...]) + _bdot(xp_ref[...].astype(BF16), cc_ref[...])
    for s in range(SSM_CHUNK):
        ya_ref[pl.ds(s, rows, stride=SSM_CHUNK), :] = y[:, 2 * s * LANES:(2 * s + 1) * LANES]
        yb_ref[pl.ds(s, rows, stride=SSM_CHUNK), :] = y[:, (2 * s + 1) * LANES:(2 * s + 2) * LANES]


def _s5(ua, ub, m, bc, cc, ar, ai, b):
    n = ua.shape[0]
    l = n // b
    rows = l // SSM_CHUNK
    half = pl.BlockSpec((l, LANES), lambda i: (i, 0))
    out = jax.ShapeDtypeStruct((n, LANES), F32)
    return pl.pallas_call(
        _s5_kernel,
        grid=(b,),
        in_specs=[half, half] + [_resident(a.shape) for a in (m, bc, cc, ar, ai)],
        out_specs=[half, half],
        out_shape=[out, out],
        scratch_shapes=[pltpu.VMEM((rows, bc.shape[1]), F32), pltpu.VMEM((rows, bc.shape[1]), F32)],
        compiler_params=pltpu.CompilerParams(dimension_semantics=("parallel",),
                                             vmem_limit_bytes=VMEM_LIMIT),
        name="s5",
    )(ua, ub, m, bc, cc, ar, ai)


def _s5_operators(a_re, a_im, log_dt, b_re, b_im, c_re, c_im, d_skip):
    t = SSM_CHUNK
    g, p = a_re.shape
    hc = d_skip.shape[-1]
    lam = lax.complex(a_re.astype(F32), a_im.astype(F32))
    dt = jnp.exp(log_dt.astype(F32))[:, None]
    a_bar = jnp.exp(lam * dt)
    b_bar = ((a_bar - 1.0) / lam)[..., None] * lax.complex(b_re.astype(F32), b_im.astype(F32))
    c = lax.complex(c_re.astype(F32), c_im.astype(F32))
    tau = jnp.arange(t + 1, dtype=F32)
    apow = jnp.exp((lam * dt)[None] * tau[:, None, None])
    eye_g = jnp.eye(g, dtype=F32)
    kern = jnp.real(jnp.einsum('gop,tgp,gpi->tgio', c, apow[:t], b_bar))
    kern = kern.at[0].add(jnp.eye(hc, dtype=F32)[None] * d_skip.astype(F32)[:, :, None])
    kb = jnp.einsum('tgio,gk->tgiko', kern, eye_g).reshape(t, g * hc, g * hc).astype(BF16)
    zero = jnp.zeros((g * hc, g * hc), BF16)
    m = jnp.concatenate([jnp.concatenate([kb[tt - ss] if tt >= ss else zero for tt in range(t)], axis=1)
                         for ss in range(t)], axis=0)
    bs = apow[:t][::-1][:, :, :, None] * b_bar[None]
    bcx = jnp.einsum('sgph,gk->sghkp', bs, eye_g.astype(bs.dtype)).reshape(t * g * hc, g * p)
    bc = jnp.concatenate([jnp.real(bcx), jnp.imag(bcx)], axis=1)
    ct = c[None] * apow[1:][:, :, None, :]
    ccx = jnp.einsum('tgop,gk->gptko', ct, eye_g.astype(ct.dtype)).reshape(g * p, t * g * hc)
    cc = jnp.concatenate([jnp.real(ccx), -jnp.imag(ccx)], axis=0)
    a_t = apow[t].reshape(1, g * p)
    return (m, bc.astype(BF16), cc.astype(BF16),
            jnp.real(a_t).astype(F32), jnp.imag(a_t).astype(F32))


def _conv_tile(h_ref, w_ref, b_ref, g_ref, beta_ref, o_ref, sh_ref, first):
    t = h_ref.shape[0]
    cw = h_ref.shape[1]

    @pl.when(first)
    def _():
        sh_ref[:, 0:CONV_HALO + 16, :] = jnp.zeros((8, CONV_HALO + 16, cw), F32)

    @pl.when(jnp.logical_not(first))
    def _():
        sh_ref[:, 0:CONV_HALO + 16, :] = sh_ref[:, t:t + CONV_HALO + 16, :]

    hv = h_ref[...].astype(F32)
    for r in range(8):
        sh_ref[r, CONV_HALO + 8 + r:CONV_HALO + 8 + r + t, :] = hv
    rb = min(CONV_ROWS, t)
    for r0 in range(0, t, rb):
        acc = jnp.zeros((rb, cw), F32)
        for j in range(CONV_K):
            delay = CONV_K - 1 - j
            base = CONV_HALO + 8 + r0 - 8 * (delay // 8)
            acc = acc + w_ref[j:j + 1, :] * sh_ref[delay % 8, base:base + rb, :]
        y = acc + b_ref[...]
        mu = jnp.mean(y, axis=-1, keepdims=True)
        yc = y - mu
        var = jnp.mean(yc * yc, axis=-1, keepdims=True)
        z = yc * lax.rsqrt(var + EPS) * g_ref[...] + beta_ref[...]
        o_ref[r0:r0 + rb, :] = _silu(z).astype(BF16)


def _pack_halves(h):
    w = h.shape[-1] // 2
    bits = lax.bitcast_convert_type(h.astype(BF16).astype(F32), jnp.uint32)
    return bits[:, :w] | (bits[:, w:] >> 16)


def _unpack_halves(u):
    hi = lax.bitcast_convert_type(u & jnp.uint32(0xFFFF0000), F32)
    lo = lax.bitcast_convert_type(u << 16, F32)
    return jnp.concatenate([hi, lo], axis=-1).astype(BF16)


def _merge_kernel(x_ref, att_ref, ya_ref, yb_ref, hc_ref, gates_ref, cw_ref, cb_ref, cg_ref, cbeta_ref,
                  wglu_ref, wa_ref, ws_ref, wc_ref, wo_ref, gf_ref, xo_ref, h_ref, sh_ref, conv_ref,
                  *, packed, tiles_per_seq):
    d = x_ref.shape[-1]
    _conv_tile(hc_ref, cw_ref, cb_ref, cg_ref, cbeta_ref, conv_ref, sh_ref,
               pl.program_id(0) % tiles_per_seq == 0)
    y = jnp.concatenate([ya_ref[...], yb_ref[...]], axis=-1)
    sw = y.shape[-1]
    yg = _bdot(_gelu_tanh(y).astype(BF16), wglu_ref[...])
    ssm = (yg[:, :sw] * _sigmoid(yg[:, sw:])).astype(BF16)
    merged = gates_ref[:, 0:d].astype(F32) * _bdot(att_ref[...], wa_ref[...])
    merged = merged + gates_ref[:, d:2 * d].astype(F32) * _bdot(ssm, ws_ref[...])
    merged = merged + gates_ref[:, 2 * d:3 * d].astype(F32) * _bdot(conv_ref[...], wc_ref[...])
    xn = x_ref[...] + _bdot(merged.astype(BF16), wo_ref[...])
    xo_ref[...] = xn
    hn = _rms(xn, gf_ref[...])
    h_ref[...] = _pack_halves(hn) if packed else hn.astype(BF16)


def _merge(x2, b, att, ya, yb, hc, gates, conv_params, wglu, wa, ws, wc, wo, gf, packed):
    n, d = x2.shape
    l = n // b
    tm = min(TOKEN_TILE, l)
    cw = hc.shape[1]
    row = lambda w: pl.BlockSpec((tm, w), lambda i: (i, 0))
    hw, hdt = (d // 2, jnp.uint32) if packed else (d, BF16)
    return pl.pallas_call(
        functools.partial(_merge_kernel, packed=packed, tiles_per_seq=l // tm),
        grid=(n // tm,),
        in_specs=[row(d), row(att.shape[1]), row(LANES), row(LANES), row(cw), row(gates.shape[1])]
                 + [_resident(a.shape) for a in (*conv_params, wglu, wa, ws, wc, wo, gf)],
        out_specs=[row(d), row(hw)],
        out_shape=[jax.ShapeDtypeStruct((n, d), F32), jax.ShapeDtypeStruct((n, hw), hdt)],
        scratch_shapes=[pltpu.VMEM((8, tm + CONV_HALO + 16, cw), F32), pltpu.VMEM((tm, cw), BF16)],
        compiler_params=pltpu.CompilerParams(dimension_semantics=("arbitrary",),
                                             vmem_limit_bytes=VMEM_LIMIT),
        name="merge",
    )(x2, att, ya, yb, hc, gates, *conv_params, wglu, wa, ws, wc, wo, gf)


def _ffn_kernel(x_ref, h_ref, wg_ref, wu_ref, wd_ref, xo_ref, *, chunks):
    h = h_ref[...]
    f = wg_ref.shape[1]
    fc = f // chunks
    acc = x_ref[...]
    for j in range(chunks):
        sl = slice(j * fc, (j + 1) * fc)
        act = (_silu(_bdot(h, wg_ref[:, sl])) * _bdot(h, wu_ref[:, sl])).astype(BF16)
        acc = acc + _bdot(act, wd_ref[sl, :])
    xo_ref[...] = acc


def _ffn(x2, h, wg, wu, wd):
    n, d = x2.shape
    tm = min(TOKEN_TILE, n)
    row = pl.BlockSpec((tm, d), lambda i: (i, 0))
    f = wg.shape[1]
    chunks = 2 if f % (2 * LANES) == 0 else 1
    return pl.pallas_call(
        functools.partial(_ffn_kernel, chunks=chunks),
        grid=(n // tm,),
        in_specs=[row, row] + [_resident(a.shape) for a in (wg, wu, wd)],
        out_specs=row,
        out_shape=jax.ShapeDtypeStruct((n, d), F32),
        compiler_params=pltpu.CompilerParams(dimension_semantics=("parallel",),
                                             vmem_limit_bytes=VMEM_LIMIT),
        name="ffn",
    )(x2, h, wg, wu, wd)


META_E1, META_E2, META_P1, META_P2, META_R1, META_R2 = range(6)


def _split2(a):
    hi = a.astype(BF16)
    return hi, (a - hi.astype(F32)).astype(BF16)


def _router_kernel(x_ref, g_ref, wr_ref, br_ref, meta_ref, counts_ref):
    @pl.when(pl.program_id(0) == 0)
    def _():
        counts_ref[...] = jnp.zeros_like(counts_ref)

    h_hi, h_lo = _split2(_rms(x_ref[...], g_ref[...]))
    w_hi, w_lo = _split2(wr_ref[...])
    logits = _bdot(h_hi, w_hi) + (_bdot(h_hi, w_lo) + _bdot(h_lo, w_hi)) + br_ref[...]
    tm = logits.shape[0]
    lane = lax.broadcasted_iota(jnp.int32, logits.shape, 1)
    logits = jnp.where(lane < N_EXPERTS, logits, NEG_BIG)
    m1 = jnp.max(logits, axis=-1, keepdims=True)
    i1 = jnp.min(jnp.where(logits == m1, lane, LANES), axis=-1, keepdims=True)
    rest = jnp.where(lane == i1, NEG_BIG, logits)
    m2 = jnp.max(rest, axis=-1, keepdims=True)
    i2 = jnp.min(jnp.where(rest == m2, lane, LANES), axis=-1, keepdims=True)
    e = jnp.exp(m2 - m1)
    p1 = 1.0 / (1.0 + e)
    p2 = e / (1.0 + e)
    assigned = jnp.where((lane == i1) | (lane == i2), 1.0, 0.0)
    r = lax.broadcasted_iota(jnp.int32, (tm, tm), 0)
    c = lax.broadcasted_iota(jnp.int32, (tm, tm), 1)
    before = _bdot((c < r).astype(BF16), assigned.astype(BF16)) + counts_ref[...]
    rank1 = jnp.sum(jnp.where(lane == i1, before, 0.0), axis=-1, keepdims=True)
    rank2 = jnp.sum(jnp.where(lane == i2, before, 0.0), axis=-1, keepdims=True)
    meta = jnp.where(lane == META_E1, i1.astype(F32), 0.0)
    for idx, val in ((META_E2, i2.astype(F32)), (META_P1, p1), (META_P2, p2), (META_R1, rank1), (META_R2, rank2)):
        meta = jnp.where(lane == idx, val, meta)
    meta_ref[...] = meta
    counts_ref[...] += jnp.sum(assigned, axis=0, keepdims=True)


def _router(x2, g, wr, br):
    n, d = x2.shape
    tm = min(TOKEN_TILE, n)
    return pl.pallas_call(
        _router_kernel,
        grid=(n // tm,),
        in_specs=[pl.BlockSpec((tm, d), lambda i: (i, 0))] + [_resident(a.shape) for a in (g, wr, br)],
        out_specs=[pl.BlockSpec((tm, LANES), lambda i: (i, 0)), pl.BlockSpec((1, LANES), lambda i: (0, 0))],
        out_shape=[jax.ShapeDtypeStruct((n, LANES), F32), jax.ShapeDtypeStruct((1, LANES), F32)],
        compiler_params=pltpu.CompilerParams(dimension_semantics=("arbitrary",),
                                             vmem_limit_bytes=VMEM_LIMIT),
        name="router",
    )(x2, g, wr, br)


def _dispatch_plan(meta, counts, tm, n_tiles):
    cnt = counts[0, :N_EXPERTS].astype(jnp.int32)
    tiles = (cnt + tm - 1) // tm
    tile_end = jnp.cumsum(tiles)
    row_start = (tile_end - tiles) * tm
    t_idx = jnp.arange(n_tiles, dtype=jnp.int32)
    n_used = tile_end[-1]
    tile_expert = jnp.sum((t_idx[:, None] >= tile_end[None, :]).astype(jnp.int32), axis=1)
    last_expert = jnp.sum((n_used - 1 >= tile_end).astype(jnp.int32))
    tile_expert = jnp.where(t_idx < n_used, tile_expert, last_expert)
    experts = jnp.arange(N_EXPERTS, dtype=jnp.int32)

    def pos(e_lane, r_lane):
        e = meta[:, e_lane].astype(jnp.int32)
        start = jnp.sum(jnp.where(e[:, None] == experts[None, :], row_start[None, :], 0), axis=1)
        return start + meta[:, r_lane].astype(jnp.int32)

    pos2 = jnp.stack([pos(META_E1, META_R1), pos(META_E2, META_R2)], axis=1)
    return pos2, tile_expert, n_used.reshape(1)


def _dispatch_kernel(pos_ref, h_ref, init_ref, sorted_ref, sem):
    del init_ref
    tm = h_ref.shape[0]

    def row_copy(r, k):
        return pltpu.make_async_copy(h_ref.at[pl.ds(r, 1), :],
                                     sorted_ref.at[pl.ds(pos_ref[0, 0, 2 * r + k], 1), :], sem)

    def start(r, carry):
        row_copy(r, 0).start()
        row_copy(r, 1).start()
        return carry

    lax.fori_loop(0, tm, start, 0, unroll=DMA_UNROLL)
    tile_copy = pltpu.make_async_copy(h_ref, sorted_ref.at[pl.ds(0, tm), :], sem)
    tile_copy.wait()
    tile_copy.wait()


def _dispatch(hp, pos2, n_rows):
    n, w = hp.shape
    tm = min(MOE_TILE, n)
    pos_tiles = pos2.reshape(n // tm, 1, 2 * tm)
    return pl.pallas_call(
        _dispatch_kernel,
        grid=(n // tm,),
        in_specs=[pl.BlockSpec((1, 1, 2 * tm), lambda i: (i, 0, 0), memory_space=pltpu.SMEM),
                  pl.BlockSpec((tm, w), lambda i: (i, 0)),
                  pl.BlockSpec(memory_space=pl.ANY)],
        out_specs=pl.BlockSpec(memory_space=pl.ANY),
        out_shape=jax.ShapeDtypeStruct((n_rows, w), hp.dtype),
        scratch_shapes=[pltpu.SemaphoreType.DMA(())],
        input_output_aliases={2: 0},
        compiler_params=pltpu.CompilerParams(dimension_semantics=("arbitrary",),
                                             vmem_limit_bytes=VMEM_LIMIT),
        name="dispatch",
    )(pos_tiles, hp, jnp.zeros((n_rows, w), hp.dtype))


def _gffn_kernel(te_ref, nu_ref, rows_ref, wg_ref, wu_ref, wd_ref, y_ref, *, chunks):
    del te_ref

    @pl.when(pl.program_id(0) < nu_ref[0])
    def _():
        h = _unpack_halves(rows_ref[...])
        fc = wg_ref.shape[3] // chunks
        acc = None
        for j in range(chunks):
            sl = slice(j * fc, (j + 1) * fc)
            act = (_silu(_bdot(h, wg_ref[0, 0, :, sl])) * _bdot(h, wu_ref[0, 0, :, sl])).astype(BF16)
            part = _bdot(act, wd_ref[0, 0, sl, :])
            acc = part if acc is None else acc + part
        y_ref[...] = acc

    @pl.when(pl.program_id(0) >= nu_ref[0])
    def _():
        y_ref[...] = jnp.zeros_like(y_ref)


def _gffn(rows, tile_expert, n_used, wg, wu, wd, layer):
    n_rows, w = rows.shape
    _, ne, d, f = wg.shape
    tm = min(MOE_TILE, n_rows)
    chunks = 2 if f % (2 * LANES) == 0 else 1
    wspec = lambda shape: pl.BlockSpec(shape, lambda t, te, nu: (layer, te[t], 0, 0),
                                       pipeline_mode=pl.Buffered(1))
    return pl.pallas_call(
        functools.partial(_gffn_kernel, chunks=chunks),
        grid_spec=pltpu.PrefetchScalarGridSpec(
            num_scalar_prefetch=2,
            grid=(n_rows // tm,),
            in_specs=[pl.BlockSpec((tm, w), lambda t, te, nu: (t, 0)),
                      wspec((1, 1, d, f)), wspec((1, 1, d, f)), wspec((1, 1, f, d))],
            out_specs=pl.BlockSpec((tm, d), lambda t, te, nu: (t, 0)),
        ),
        out_shape=jax.ShapeDtypeStruct((n_rows, d), F32),
        compiler_params=pltpu.CompilerParams(dimension_semantics=("arbitrary",),
                                             vmem_limit_bytes=VMEM_LIMIT),
        name="grouped_ffn",
    )(tile_expert, n_used, rows, wg, wu, wd)


def _combine_kernel(pos_ref, x_ref, meta_ref, g_ref, y_ref, o_ref, buf1_ref, buf2_ref, sem, *, final):
    tm = x_ref.shape[0]
    bufs = (buf1_ref, buf2_ref)

    def row_copy(r, k):
        return pltpu.make_async_copy(y_ref.at[pl.ds(pos_ref[0, 0, 2 * r + k], 1), :],
                                     bufs[k].at[pl.ds(r, 1), :], sem)

    def start(r, carry):
        row_copy(r, 0).start()
        row_copy(r, 1).start()
        return carry

    lax.fori_loop(0, tm, start, 0, unroll=DMA_UNROLL)
    for buf in bufs:
        pltpu.make_async_copy(y_ref.at[pl.ds(0, tm), :], buf, sem).wait()
    meta = meta_ref[...]
    p1 = meta[:, META_P1:META_P1 + 1]
    p2 = meta[:, META_P2:META_P2 + 1]
    xn = x_ref[...] + p1 * buf1_ref[...] + p2 * buf2_ref[...]
    o_ref[...] = _rms(xn, g_ref[...]) if final else xn


def _combine(x2, meta, pos2, y, g, final):
    n, d = x2.shape
    tm = min(MOE_TILE, n)
    pos_tiles = pos2.reshape(n // tm, 1, 2 * tm)
    return pl.pallas_call(
        functools.partial(_combine_kernel, final=final),
        grid=(n // tm,),
        in_specs=[pl.BlockSpec((1, 1, 2 * tm), lambda i: (i, 0, 0), memory_space=pltpu.SMEM),
                  pl.BlockSpec((tm, d), lambda i: (i, 0)),
                  pl.BlockSpec((tm, LANES), lambda i: (i, 0)),
                  _resident(g.shape),
                  pl.BlockSpec(memory_space=pl.ANY)],
        out_specs=pl.BlockSpec((tm, d), lambda i: (i, 0)),
        out_shape=jax.ShapeDtypeStruct((n, d), F32),
        scratch_shapes=[pltpu.VMEM((tm, d), F32), pltpu.VMEM((tm, d), F32), pltpu.SemaphoreType.DMA(())],
        compiler_params=pltpu.CompilerParams(dimension_semantics=("arbitrary",),
                                             vmem_limit_bytes=VMEM_LIMIT),
        name="combine",
    )(pos_tiles, x2, meta, g, y)


def _moe(x2, hp, g_ffn_row, wr, br, wg, wu, wd, layer, g_out, final):
    n, _ = x2.shape
    tm = min(MOE_TILE, n)
    n_tiles = 2 * n // tm + N_EXPERTS
    meta, counts = _router(x2, g_ffn_row, wr, br)
    pos2, tile_expert, n_used = _dispatch_plan(meta, counts, tm, n_tiles)
    rows = _dispatch(hp, pos2, n_tiles * tm)
    y = _gffn(rows, tile_expert, n_used, wg, wu, wd, layer)
    return _combine(x2, meta, pos2, y, g_out, final)


def _final_kernel(x_ref, g_ref, o_ref):
    o_ref[...] = _rms(x_ref[...], g_ref[...])


def _final_norm(x2, g):
    n, d = x2.shape
    tm = min(TOKEN_TILE, n)
    row = pl.BlockSpec((tm, d), lambda i: (i, 0))
    return pl.pallas_call(
        _final_kernel,
        grid=(n // tm,),
        in_specs=[row, _resident(g.shape)],
        out_specs=row,
        out_shape=jax.ShapeDtypeStruct((n, d), F32),
        compiler_params=pltpu.CompilerParams(dimension_semantics=("parallel",),
                                             vmem_limit_bytes=VMEM_LIMIT),
        name="final_norm",
    )(x2, g)


def _pad_cols(a, width):
    return jnp.pad(a, ((0, 0), (0, width - a.shape[1])))


def kernel(x, g_mix, w_in, b_in, ssm_a_re, ssm_a_im, ssm_log_dt, ssm_b_re, ssm_b_im, ssm_c_re, ssm_c_im, ssm_d, ssm_w_glu, conv_w, conv_b, conv_ln_g, conv_ln_b, w_att_out, w_ssm_out, w_conv_out, w_o, g_ffn, ffn_w_gate, ffn_w_up, ffn_w_down, moe_w_router, moe_b_router, moe_w_gate, moe_w_up, moe_w_down, g_final):
    b, l, d = x.shape
    depth = w_in.shape[0]
    n = b * l
    aw = ATT_HEADS * ATT_HEAD_DIM
    sw = SSM_GROUPS * SSM_GROUP
    cw = conv_w.shape[-1]
    q_end, k_end, v_end = aw, 2 * aw, 3 * aw
    f_end = v_end + ATT_HEADS
    u_end = f_end + sw
    c_end = u_end + 2 * cw
    scale = ATT_HEAD_DIM ** -0.5
    row2 = lambda v: v.reshape(1, -1).astype(F32)

    moe_wg, moe_wu, moe_wd = (a.astype(BF16) for a in (moe_w_gate, moe_w_up, moe_w_down))
    x2 = x.reshape(n, d)
    for layer in range(depth):
        w = w_in[layer]
        bias = b_in[layer]
        pad = lambda v: _pad_cols(v, LANES)
        col = lambda v: v.reshape(-1, 1).astype(F32)
        weights = (row2(g_mix[layer]),
                   (w[:, :q_end] * scale).astype(BF16), row2(bias[:q_end] * scale),
                   w[:, q_end:k_end].astype(BF16), row2(bias[q_end:k_end]),
                   w[:, k_end:v_end].T.astype(BF16), col(bias[k_end:v_end]),
                   pad(w[:, v_end:f_end]).astype(BF16), pad(row2(bias[v_end:f_end])),
                   w[:, f_end:u_end].astype(BF16), row2(bias[f_end:u_end]),
                   w[:, u_end:c_end].astype(BF16), row2(bias[u_end:c_end]),
                   w[:, c_end:].astype(BF16), row2(bias[c_end:]))
        qa, ka, vt, ua, ub, hc, gates = _inproj(x2, b, weights)
        att = _fox_attention(qa, ka, vt).reshape(n, aw)

        m, bc, cc, ar, ai = _s5_operators(ssm_a_re[layer], ssm_a_im[layer], ssm_log_dt[layer],
                                          ssm_b_re[layer], ssm_b_im[layer], ssm_c_re[layer],
                                          ssm_c_im[layer], ssm_d[layer])
        ya, yb = _s5(ua, ub, m, bc, cc, ar, ai, b)

        wconv = jnp.pad(conv_w[layer].astype(F32), ((0, CONV_HALO - CONV_K), (0, 0)))
        conv_params = (wconv, row2(conv_b[layer]), row2(conv_ln_g[layer]), row2(conv_ln_b[layer]))

        is_moe = layer % 2 == 1
        last = layer == depth - 1
        x2, h = _merge(x2, b, att, ya, yb, hc, gates, conv_params, ssm_w_glu[layer].astype(BF16),
                       w_att_out[layer].astype(BF16), w_ssm_out[layer].astype(BF16),
                       w_conv_out[layer].astype(BF16), w_o[layer].astype(BF16), row2(g_ffn[layer]), is_moe)

        i = layer // 2
        if is_moe:
            x2 = _moe(x2, h, row2(g_ffn[layer]), _pad_cols(moe_w_router[i].astype(F32), LANES),
                      _pad_cols(row2(moe_b_router[i]), LANES), moe_wg, moe_wu, moe_wd, i,
                      row2(g_final), last)
        else:
            x2 = _ffn(x2, h, ffn_w_gate[i].astype(BF16), ffn_w_up[i].astype(BF16), ffn_w_down[i].astype(BF16))
            if last:
                x2 = _final_norm(x2, row2(g_final))
    return x2.reshape(b, l, d)
```
